```python
import jax
import jax.numpy as jnp
from jax import lax
import numpy as np

D_MODEL = 1024
BATCH = 2
SEQ = 8192
DEPTH = 2

RMS_EPS = 1e-6
MIX_W = D_MODEL
A_HEADS = 4
A_HEAD_K = 128
A_HEAD_V = MIX_W // 2 // A_HEADS
A_KW = A_HEADS * A_HEAD_K
A_W = A_HEADS * A_HEAD_V
HGRN_CHUNK = 64
B_GROUPS = 4
B_W = MIX_W // 2
B_GROUP_DIM = B_W // B_GROUPS
GMLP_CHUNK = 128
EVEN_IN = 2 * A_KW + 2 * A_W + 2 * B_W
C_HEADS = 4
Q_LORA = 256
KV_LORA = 128
C_NOPE = 128
C_ROPE = 64
C_QK = C_NOPE + C_ROPE
C_V = MIX_W // 2 // C_HEADS
C_W = C_HEADS * C_V
ROPE_THETA = 10000.0
ATTN_Q_BLOCK = 128
D_HEADS = 4
D_HEAD_DIM = MIX_W // 2 // D_HEADS
D_W = D_HEADS * D_HEAD_DIM
MOBA_BLOCK = 256
MOBA_TOPK = 3
MOBA_Q_CHUNK = 64
ODD_IN = Q_LORA + KV_LORA + C_ROPE + 3 * D_W
PEER_HEADS = 8
PEER_N_KEYS = 128
PEER_N_EXPERTS = PEER_N_KEYS * PEER_N_KEYS
PEER_TOPK = 16
PEER_HALF = 128
PEER_QDIM = 2 * PEER_HALF
PEER_CHUNK = 128

kernel_name = 'hybrid_hgrn2_gmlp_mla_moba_peer'


def rmsnorm(x, g):
    xf = x.astype(jnp.float32)
    y = xf * lax.rsqrt(jnp.mean(xf * xf, axis=-1, keepdims=True) + RMS_EPS)
    return y.astype(x.dtype) * g


def split_cols(z, sizes):
    cuts = [int(c) for c in np.cumsum(sizes)[:-1]]
    return jnp.split(z, cuts, axis=-1)


def rope(x, positions):
    half = x.shape[-1] // 2
    inv_freq = ROPE_THETA ** (-jnp.arange(half, dtype=jnp.float32) / half)
    ang = positions.astype(jnp.float32)[..., None] * inv_freq
    cos = jnp.cos(ang)[:, :, None, :]
    sin = jnp.sin(ang)[:, :, None, :]
    x1 = x[..., :half].astype(jnp.float32)
    x2 = x[..., half:].astype(jnp.float32)
    return jnp.concatenate([x1 * cos - x2 * sin, x2 * cos + x1 * sin], axis=-1).astype(x.dtype)


def hgrn_lower_bounds(lb_logits):
    p = jax.nn.softmax(lb_logits.astype(jnp.float32), axis=0)
    return jnp.cumsum(p, axis=0)[:DEPTH]


def hgrn2_recurrence(q, k, v, log_f):
    B, S, H, dk = q.shape
    dv = v.shape[-1]
    nc = S // HGRN_CHUNK

    def chunks(t):
        return t.reshape(B, nc, HGRN_CHUNK, H, t.shape[-1]).transpose(1, 0, 3, 2, 4)

    causal = jnp.tril(jnp.ones((HGRN_CHUNK, HGRN_CHUNK), dtype=bool))[:, :, None]

    def step(state, inp):
        qi, ki, vi, gi = inp
        b = jnp.cumsum(gi.astype(jnp.float32), axis=2)
        o_inter = jnp.einsum('bhtk,bhkv->bhtv', qi * jnp.exp(b), state)
        diff = b[:, :, :, None, :] - b[:, :, None, :, :]
        decay = jnp.exp(jnp.where(causal, diff, -jnp.inf))
        scores = jnp.einsum('bhtk,bhsk,bhtsk->bhts', qi, ki, decay)
        o = o_inter + jnp.einsum('bhts,bhsv->bhtv', scores, vi)
        b_last = b[:, :, -1:, :]
        k_dec = ki * jnp.exp(b_last - b)
        state = jnp.exp(b_last[:, :, 0, :])[..., None] * state + jnp.einsum('bhsk,bhsv->bhkv', k_dec, vi)
        return state, o

    state0 = jnp.zeros((B, H, dk, dv), jnp.float32)
    _, o = lax.scan(step, state0, (chunks(q), chunks(k), chunks(v), chunks(log_f)))
    return o.transpose(1, 0, 3, 2, 4).reshape(B, S, H, dv)


def causal_attention(q, k, v, scale):
    B, H, S, dqk = q.shape
    nq = S // ATTN_Q_BLOCK
    qb = q.reshape(B, H, nq, ATTN_Q_BLOCK, dqk).transpose(2, 0, 1, 3, 4)
    k_pos = jnp.arange(S)

    def block(args):
        i, qi = args
        s = jnp.einsum('bhqd,bhkd->bhqk', qi, k).astype(jnp.float32) * scale
        q_pos = i * ATTN_Q_BLOCK + jnp.arange(ATTN_Q_BLOCK)
        s = jnp.where(k_pos[None, :] <= q_pos[:, None], s, -jnp.inf)
        p = jax.nn.softmax(s, axis=-1).astype(v.dtype)
        return jnp.einsum('bhqk,bhkd->bhqd', p, v)

    o = lax.map(block, (jnp.arange(nq), qb))
    return o.transpose(1, 2, 0, 3, 4).reshape(B, H, S, v.shape[-1])


def moba_attention(q, k, v, scale):
    B, H, S, d = q.shape
    nb = -(-S // MOBA_BLOCK)
    sp = nb * MOBA_BLOCK
    pad = ((0, 0), (0, 0), (0, sp - S), (0, 0))
    q, k, v = jnp.pad(q, pad), jnp.pad(k, pad), jnp.pad(v, pad)
    kb = k.reshape(B, H, nb, MOBA_BLOCK, d)
    vb = v.reshape(B, H, nb, MOBA_BLOCK, d)
    k_mean = jnp.mean(kb.astype(jnp.float32), axis=3)
    gate = jnp.einsum('bhsd,bhnd->bhsn', q.astype(jnp.float32), k_mean)
    q_block = jnp.arange(sp) // MOBA_BLOCK
    past = jnp.arange(nb)[None, :] < q_block[:, None]
    gate = jnp.where(past, gate, -jnp.inf)
    topk = min(MOBA_TOPK, nb)
    _, sel = lax.top_k(gate, topk)
    valid = sel < q_block[:, None]
    nqc = sp // MOBA_Q_CHUNK

    def chunks(t):
        return t.reshape(B, H, nqc, MOBA_Q_CHUNK, t.shape[-1]).transpose(2, 0, 1, 3, 4)

    b_idx = jnp.arange(B)[:, None, None, None]
    h_idx = jnp.arange(H)[None, :, None, None]

    def chunk(args):
        i, qi, si, vi_ = args
        k_sel = kb[b_idx, h_idx, si]
        v_sel = vb[b_idx, h_idx, si]
        s_sel = jnp.einsum('bhqd,bhqnkd->bhqnk', qi, k_sel).astype(jnp.float32) * scale
        s_sel = jnp.where(vi_[..., None], s_sel, -jnp.inf).reshape(B, H, MOBA_Q_CHUNK, topk * MOBA_BLOCK)
        j = (i * MOBA_Q_CHUNK) // MOBA_BLOCK
        k_own = lax.dynamic_index_in_dim(kb, j, axis=2, keepdims=False)
        v_own = lax.dynamic_index_in_dim(vb, j, axis=2, keepdims=False)
        s_own = jnp.einsum('bhqd,bhkd->bhqk', qi, k_own).astype(jnp.float32) * scale
        q_pos = i * MOBA_Q_CHUNK + jnp.arange(MOBA_Q_CHUNK)
        k_pos = j * MOBA_BLOCK + jnp.arange(MOBA_BLOCK)
        s_own = jnp.where(k_pos[None, :] <= q_pos[:, None], s_own, -jnp.inf)
        p = jax.nn.softmax(jnp.concatenate([s_sel, s_own], axis=-1), axis=-1).astype(v.dtype)
        p_sel = p[..., :topk * MOBA_BLOCK].reshape(B, H, MOBA_Q_CHUNK, topk, MOBA_BLOCK)
        p_own = p[..., topk * MOBA_BLOCK:]
        return (jnp.einsum('bhqnk,bhqnkd->bhqd', p_sel, v_sel)
                + jnp.einsum('bhqk,bhkd->bhqd', p_own, v_own))

    o = lax.map(chunk, (jnp.arange(nqc), chunks(q), chunks(sel), chunks(valid)))
    return o.transpose(1, 2, 0, 3, 4).reshape(B, H, sp, d)[:, :, :S]


def hgrn_gmlp_mixer(x, lb, norm_mix, w_in, hgrn_out_norm, gmlp_v_norm, gmlp_w_s, gmlp_b_s, w_out):
    B, S, _ = x.shape
    xn = rmsnorm(x, norm_mix)
    q, fg, inp, og, u, v = split_cols(xn @ w_in, [A_KW, A_KW, A_W, A_W, B_W, B_W])
    f = lb + (1.0 - lb) * jax.nn.sigmoid(fg.astype(jnp.float32))
    o = hgrn2_recurrence(q.reshape(B, S, A_HEADS, A_HEAD_K),
                         (1.0 - f).reshape(B, S, A_HEADS, A_HEAD_K),
                         jax.nn.silu(inp).reshape(B, S, A_HEADS, A_HEAD_V),
                         jnp.log(f).reshape(B, S, A_HEADS, A_HEAD_K))
    o = rmsnorm(o.astype(x.dtype), hgrn_out_norm.reshape(A_HEADS, A_HEAD_V))
    out_a = o.reshape(B, S, A_W) * jax.nn.silu(og)
    u = jax.nn.gelu(u)
    v = rmsnorm(jax.nn.gelu(v).reshape(B, S, B_GROUPS, B_GROUP_DIM),
                gmlp_v_norm.reshape(B_GROUPS, B_GROUP_DIM))
    nc = S // GMLP_CHUNK
    w_causal = gmlp_w_s * jnp.tril(jnp.ones((GMLP_CHUNK, GMLP_CHUNK), gmlp_w_s.dtype))
    sv = jnp.einsum('gts,bnsgd->bntgd', w_causal, v.reshape(B, nc, GMLP_CHUNK, B_GROUPS, B_GROUP_DIM))
    sv = sv + gmlp_b_s.T[:, :, None]
    out_b = u * sv.reshape(B, S, B_W)
    return jnp.concatenate([out_a, out_b], axis=-1) @ w_out


def mla_moba_mixer(x, positions, norm_mix, w_in, cq_norm, ckv_norm, w_uq, w_ukv,
                   mla_q_norm, mla_k_norm, moba_q_norm, moba_k_norm, w_out):
    B, S, _ = x.shape
    xn = rmsnorm(x, norm_mix)
    c_q, c_kv, k_pe, q_d, k_d, v_d = split_cols(xn @ w_in, [Q_LORA, KV_LORA, C_ROPE, D_W, D_W, D_W])
    q_c = (rmsnorm(c_q, cq_norm) @ w_uq).reshape(B, S, C_HEADS, C_QK)
    kv_c = (rmsnorm(c_kv, ckv_norm) @ w_ukv).reshape(B, S, C_HEADS, C_NOPE + C_V)
    k_c = jnp.concatenate([kv_c[..., :C_NOPE],
                           jnp.broadcast_to(k_pe[:, :, None, :], (B, S, C_HEADS, C_ROPE))], axis=-1)
    v_c = kv_c[..., C_NOPE:]
    q_c = rmsnorm(q_c, mla_q_norm)
    k_c = rmsnorm(k_c, mla_k_norm)
    q_c = jnp.concatenate([q_c[..., :C_NOPE], rope(q_c[..., C_NOPE:], positions)], axis=-1)
    k_c = jnp.concatenate([k_c[..., :C_NOPE], rope(k_c[..., C_NOPE:], positions)], axis=-1)
    o_c = causal_attention(q_c.transpose(0, 2, 1, 3), k_c.transpose(0, 2, 1, 3),
                           v_c.transpose(0, 2, 1, 3), C_QK ** -0.5)
    out_c = o_c.transpose(0, 2, 1, 3).reshape(B, S, C_W)
    q_d = rmsnorm(q_d.reshape(B, S, D_HEADS, D_HEAD_DIM), moba_q_norm)
    k_d = rmsnorm(k_d.reshape(B, S, D_HEADS, D_HEAD_DIM), moba_k_norm)
    v_d = v_d.reshape(B, S, D_HEADS, D_HEAD_DIM)
    o_d = moba_attention(q_d.transpose(0, 2, 1, 3), k_d.transpose(0, 2, 1, 3),
                         v_d.transpose(0, 2, 1, 3), D_HEAD_DIM ** -0.5)
    out_d = o_d.transpose(0, 2, 1, 3).reshape(B, S, D_W)
    return jnp.concatenate([out_c, out_d], axis=-1) @ w_out


def peer_ffn(x, norm_ffn, w_query, sub_keys, expert_down, expert_up):
    B, S, D = x.shape
    xn = rmsnorm(x, norm_ffn)
    nt = S // PEER_CHUNK
    xc = xn.reshape(B, nt, PEER_CHUNK, D).transpose(1, 0, 2, 3)

    def chunk(xi):
        q = (xi @ w_query).reshape(B, PEER_CHUNK, PEER_HEADS, 2, PEER_HALF)
        s = jnp.einsum('bthpd,hpnd->bthpn', q, sub_keys)
        s1, i1 = lax.top_k(s[..., 0, :], PEER_TOPK)
        s2, i2 = lax.top_k(s[..., 1, :], PEER_TOPK)
        cand = (s1[..., :, None] + s2[..., None, :]).reshape(B, PEER_CHUNK, PEER_HEADS, PEER_TOPK * PEER_TOPK)
        top_s, top_c = lax.top_k(cand, PEER_TOPK)
        idx = (jnp.take_along_axis(i1, top_c // PEER_TOPK, axis=-1) * PEER_N_KEYS
               + jnp.take_along_axis(i2, top_c % PEER_TOPK, axis=-1))
        g = jax.nn.softmax(top_s.astype(jnp.float32), axis=-1)
        u = jnp.take(expert_down, idx, axis=0)
        act = jax.nn.gelu(jnp.einsum('btd,bthkd->bthk', xi, u).astype(jnp.float32))
        v_sel = jnp.take(expert_up, idx, axis=0)
        return jnp.einsum('bthk,bthkd->btd', (g * act).astype(xi.dtype), v_sel)

    out = lax.map(chunk, xc)
    return out.transpose(1, 0, 2, 3).reshape(B, S, D)


def setup_inputs(seed: int = 0) -> dict:
    key = jax.random.key(seed)
    ks = iter(jax.random.split(key, 64))

    def nrm(shape, scale):
        return jax.random.normal(next(ks), shape, jnp.float32) * scale

    def gain(n):
        return 1.0 + 0.1 * jax.random.normal(next(ks), (n,), jnp.float32)

    d = D_MODEL
    inp = {}
    inp['x'] = nrm((BATCH, SEQ, d), 1.0)
    inp['positions'] = jnp.broadcast_to(jnp.arange(SEQ, dtype=jnp.int32), (BATCH, SEQ))
    inp['lb_logits'] = nrm((DEPTH + 1, A_KW), 0.5)
    inp['l0_norm_mix'] = gain(d)
    inp['l0_w_in'] = nrm((d, EVEN_IN), d ** -0.5)
    inp['l0_hgrn_out_norm'] = gain(A_W)
    inp['l0_gmlp_v_norm'] = gain(B_W)
    inp['l0_gmlp_w_s'] = nrm((B_GROUPS, GMLP_CHUNK, GMLP_CHUNK), GMLP_CHUNK ** -0.5)
    inp['l0_gmlp_b_s'] = 1.0 + nrm((B_GROUPS, GMLP_CHUNK), 0.1)
    inp['l0_w_out'] = nrm((A_W + B_W, d), (A_W + B_W) ** -0.5)
    inp['l0_norm_ffn'] = gain(d)
    inp['l0_peer_w_query'] = nrm((d, PEER_HEADS * PEER_QDIM), d ** -0.5)
    inp['l0_peer_sub_keys'] = nrm((PEER_HEADS, 2, PEER_N_KEYS, PEER_HALF), PEER_HALF ** -0.5)
    inp['l0_peer_expert_down'] = nrm((PEER_N_EXPERTS, d), d ** -0.5)
    inp['l0_peer_expert_up'] = nrm((PEER_N_EXPERTS, d), 0.25)
    inp['l1_norm_mix'] = gain(d)
    inp['l1_w_in'] = nrm((d, ODD_IN), d ** -0.5)
    inp['l1_mla_cq_norm'] = gain(Q_LORA)
    inp['l1_mla_ckv_norm'] = gain(KV_LORA)
    inp['l1_mla_w_uq'] = nrm((Q_LORA, C_HEADS * C_QK), Q_LORA ** -0.5)
    inp['l1_mla_w_ukv'] = nrm((KV_LORA, C_HEADS * (C_NOPE + C_V)), KV_LORA ** -0.5)
    inp['l1_mla_q_norm'] = gain(C_QK)
    inp['l1_mla_k_norm'] = gain(C_QK)
    inp['l1_moba_q_norm'] = gain(D_HEAD_DIM)
    inp['l1_moba_k_norm'] = gain(D_HEAD_DIM)
    inp['l1_w_out'] = nrm((C_W + D_W, d), (C_W + D_W) ** -0.5)
    inp['l1_norm_ffn'] = gain(d)
    inp['l1_peer_w_query'] = nrm((d, PEER_HEADS * PEER_QDIM), d ** -0.5)
    inp['l1_peer_sub_keys'] = nrm((PEER_HEADS, 2, PEER_N_KEYS, PEER_HALF), PEER_HALF ** -0.5)
    inp['l1_peer_expert_down'] = nrm((PEER_N_EXPERTS, d), d ** -0.5)
    inp['l1_peer_expert_up'] = nrm((PEER_N_EXPERTS, d), 0.25)
    return inp


def reference(x, positions, lb_logits,
              l0_norm_mix, l0_w_in, l0_hgrn_out_norm, l0_gmlp_v_norm, l0_gmlp_w_s, l0_gmlp_b_s, l0_w_out,
              l0_norm_ffn, l0_peer_w_query, l0_peer_sub_keys, l0_peer_expert_down, l0_peer_expert_up,
              l1_norm_mix, l1_w_in, l1_mla_cq_norm, l1_mla_ckv_norm, l1_mla_w_uq, l1_mla_w_ukv,
              l1_mla_q_norm, l1_mla_k_norm, l1_moba_q_norm, l1_moba_k_norm, l1_w_out,
              l1_norm_ffn, l1_peer_w_query, l1_peer_sub_keys, l1_peer_expert_down, l1_peer_expert_up):
    lbs = hgrn_lower_bounds(lb_logits)
    mixer_params = [
        (lbs[0], l0_norm_mix, l0_w_in, l0_hgrn_out_norm, l0_gmlp_v_norm, l0_gmlp_w_s, l0_gmlp_b_s, l0_w_out),
        (positions, l1_norm_mix, l1_w_in, l1_mla_cq_norm, l1_mla_ckv_norm, l1_mla_w_uq, l1_mla_w_ukv,
         l1_mla_q_norm, l1_mla_k_norm, l1_moba_q_norm, l1_moba_k_norm, l1_w_out),
    ]
    ffn_params = [
        (l0_norm_ffn, l0_peer_w_query, l0_peer_sub_keys, l0_peer_expert_down, l0_peer_expert_up),
        (l1_norm_ffn, l1_peer_w_query, l1_peer_sub_keys, l1_peer_expert_down, l1_peer_expert_up),
    ]
    for layer in range(DEPTH):
        mixer = hgrn_gmlp_mixer if layer % 2 == 0 else mla_moba_mixer
        x = x + mixer(x, *mixer_params[layer])
        x = x + peer_ffn(x, *ffn_params[layer])
    return x
```

```python
import functools

import numpy as np
import jax
import jax.numpy as jnp
from jax import lax
from jax.experimental import pallas as pl
from jax.experimental.pallas import tpu as pltpu

F32 = jnp.float32
BF16 = jnp.bfloat16

RMS_EPS = 1e-6
LANES = 128
VMEM_LIMIT = 56 * 1024 * 1024

HGRN_HEADS = 4
GMLP_GROUPS = 4
MIX_CHUNK = 128
MLA_HEADS = 4
MLA_NOPE = 128
MLA_ROPE = 64
MLA_QK = MLA_NOPE + MLA_ROPE
ROPE_THETA = 10000.0
MOBA_HEADS = 4
MOBA_BLOCK = 256
MOBA_TOPK = 3
PEER_HEADS = 8
PEER_KEYS = 128
PEER_TOPK = 16
NEG_BIG = -1e30


def _cparams(semantics):
    return pltpu.CompilerParams(dimension_semantics=semantics, vmem_limit_bytes=VMEM_LIMIT)


def _dot(a, b):
    return jnp.dot(a, b, preferred_element_type=F32)


def _dot_nt(a, b):
    return lax.dot_general(a, b, (((1,), (1,)), ((), ())), preferred_element_type=F32)


def _dot_tn(a, b):
    return lax.dot_general(a, b, (((0,), (0,)), ((), ())), preferred_element_type=F32)


def _rms(x, gain):
    return x * lax.rsqrt(jnp.mean(x * x, axis=-1, keepdims=True) + RMS_EPS) * gain


def _gelu(x):
    c = np.float32(np.sqrt(2.0 / np.pi))
    return x * (0.5 * (1.0 + jnp.tanh(c * (x + 0.044715 * (x * x * x)))))


def _silu(x):
    return x * jax.nn.sigmoid(x)


def _norm_matmul_body(x_ref, g_ref, w_ref, o_ref):
    y = _rms(x_ref[...], g_ref[...])
    o_ref[...] = _dot(y.astype(BF16), w_ref[...])


def _norm_matmul(x2d, gain, w_bf16, tm=512):
    n, d = x2d.shape
    m = w_bf16.shape[1]
    return pl.pallas_call(
        _norm_matmul_body,
        grid=(n // tm,),
        in_specs=[pl.BlockSpec((tm, d), lambda i: (i, 0)),
                  pl.BlockSpec((1, d), lambda i: (0, 0)),
                  pl.BlockSpec((d, m), lambda i: (0, 0))],
        out_specs=pl.BlockSpec((tm, m), lambda i: (i, 0)),
        out_shape=jax.ShapeDtypeStruct((n, m), F32),
        compiler_params=_cparams(("parallel",)),
    )(x2d, gain.reshape(1, d), w_bf16)


def _out_proj_body(x_ref, a_ref, b_ref, wa_ref, wb_ref, o_ref):
    o_ref[...] = (x_ref[...] + _dot(a_ref[...].astype(BF16), wa_ref[...])
                  + _dot(b_ref[...].astype(BF16), wb_ref[...]))


def _out_proj(x2d, a, b, wa_bf16, wb_bf16, tm=512):
    n, d = x2d.shape
    ka, kb = a.shape[1], b.shape[1]
    return pl.pallas_call(
        _out_proj_body,
        grid=(n // tm,),
        in_specs=[pl.BlockSpec((tm, d), lambda i: (i, 0)),
                  pl.BlockSpec((tm, ka), lambda i: (i, 0)),
                  pl.BlockSpec((tm, kb), lambda i: (i, 0)),
                  pl.BlockSpec((ka, d), lambda i: (0, 0)),
                  pl.BlockSpec((kb, d), lambda i: (0, 0))],
        out_specs=pl.BlockSpec((tm, d), lambda i: (i, 0)),
        out_shape=jax.ShapeDtypeStruct((n, d), F32),
        compiler_params=_cparams(("parallel",)),
    )(x2d, a, b, wa_bf16, wb_bf16)


_HGRN_LEVELS = (64, 32, 16, 8, 4, 2, 1)


def _hgrn_exponent_matrix(c):
    a = np.zeros((len(_HGRN_LEVELS) + 1, c, c), np.float32)
    r = np.arange(c)[:, None]
    u = np.arange(c)[None, :]
    a[0] = (u <= r)
    for li, w in enumerate(_HGRN_LEVELS):
        p = (r // (2 * w)) * (2 * w) + w
        second = r >= p
        a[li + 1] = np.where(second, (u >= p) & (u <= r), (u > r) & (u <= p - 1))
    return a.reshape(-1, c)


def _l0_mix_body(q_ref, fg_ref, inp_ref, og_ref, u_ref, v_ref, lb_ref, on_ref, vn_ref,
                 ws_ref, bs_ref, amat_ref, oa_ref, ob_ref, state_ref):
    c = MIX_CHUNK

    @pl.when(pl.program_id(2) == 0)
    def _():
        state_ref[...] = jnp.zeros_like(state_ref)

    row = lax.broadcasted_iota(jnp.int32, (c, c), 0)
    col = lax.broadcasted_iota(jnp.int32, (c, c), 1)
    rowc = lax.broadcasted_iota(jnp.int32, (c, 1), 0)

    q = q_ref[...]
    lb = lb_ref[...]
    f = lb + (1.0 - lb) * jax.nn.sigmoid(fg_ref[...])
    k = 1.0 - f
    g = jnp.log(f)
    val = _silu(inp_ref[...])
    g1 = g.astype(BF16)
    r1 = g - g1.astype(F32)
    g2 = r1.astype(BF16)
    g3 = (r1 - g2.astype(F32)).astype(BF16)
    amat = amat_ref[...]
    dall = _dot(amat, g1) + _dot(amat, g2) + _dot(amat, g3)
    b = dall[0:c]

    scores = jnp.where(row == col, _dot_nt(q.astype(BF16), k.astype(BF16)), 0.0)
    for li, w in enumerate(_HGRN_LEVELS):
        e = jnp.exp(dall[(li + 1) * c:(li + 2) * c])
        sh = int(np.log2(w))
        second = (jnp.right_shift(rowc, sh) & 1) == 1
        qt = jnp.where(second, q * e, 0.0).astype(BF16)
        kt = jnp.where(second, 0.0, k * e).astype(BF16)
        same = jnp.right_shift(row, sh + 1) == jnp.right_shift(col, sh + 1)
        scores = scores + jnp.where(same, _dot_nt(qt, kt), 0.0)
    val16 = val.astype(BF16)
    o = _dot(scores.astype(BF16), val16)
    state_t = state_ref[...]
    o = o + _dot_nt((q * jnp.exp(b)).astype(BF16), state_t.astype(BF16))
    b_last = b[c - 1:c]
    k_dec = (k * jnp.exp(b_last - b)).astype(BF16)
    state_ref[...] = state_t * jnp.exp(b_last) + _dot_tn(val16, k_dec)
    oa_ref[...] = _rms(o, on_ref[...]) * _silu(og_ref[...])

    u = _gelu(u_ref[...])
    vn = _rms(_gelu(v_ref[...]), vn_ref[...])
    w_causal = jnp.where(row >= col, ws_ref[0], 0.0).astype(BF16)
    sv = _dot(w_causal, vn.astype(BF16)) + bs_ref[0]
    ob_ref[...] = u * sv


def _l0_mixer(z, lb, out_norm, v_norm, w_s, b_s, batch, seq):
    n = z.shape[0]
    c = MIX_CHUNK
    nc = seq // c
    h = HGRN_HEADS
    amat = jnp.asarray(_hgrn_exponent_matrix(c), BF16)
    na = amat.shape[0]

    def zcol(group):
        return pl.BlockSpec((c, LANES), lambda b, hh, i, g=group: (b * nc + i, g * h + hh))

    def vec():
        return pl.BlockSpec((1, LANES), lambda b, hh, i: (0, hh))

    out_spec = pl.BlockSpec((c, LANES), lambda b, hh, i: (b * nc + i, hh))
    return pl.pallas_call(
        _l0_mix_body,
        grid=(batch, h, nc),
        in_specs=[zcol(0), zcol(1), zcol(2), zcol(3), zcol(4), zcol(5),
                  vec(), vec(), vec(),
                  pl.BlockSpec((1, c, c), lambda b, hh, i: (hh, 0, 0)),
                  pl.BlockSpec((1, c, 1), lambda b, hh, i: (hh, 0, 0)),
                  pl.BlockSpec((na, c), lambda b, hh, i: (0, 0))],
        out_specs=[out_spec, out_spec],
        out_shape=[jax.ShapeDtypeStruct((n, h * LANES), F32),
                   jax.ShapeDtypeStruct((n, h * LANES), F32)],
        scratch_shapes=[pltpu.VMEM((LANES, LANES), F32)],
        compiler_params=_cparams(("parallel", "parallel", "arbitrary")),
    )(z, z, z, z, z, z, lb.reshape(1, -1), out_norm.reshape(1, -1), v_norm.reshape(1, -1),
      w_s, b_s.reshape(GMLP_GROUPS, c, 1), amat)


def _mla_prep_body(cq_ref, ckv_ref, kpe_ref, kpes_ref, cos_ref, sin_ref, cqn_ref, ckvn_ref,
                   wuq_ref, wukv_ref, gq_ref, gk_ref, q_out, k_out, v_out):
    cos = cos_ref[...]
    sin = sin_ref[...]
    qn = _rms(cq_ref[...], cqn_ref[...])
    kvn = _rms(ckv_ref[...], ckvn_ref[...])
    qall = _dot(qn.astype(BF16), wuq_ref[...])
    kvall = _dot(kvn.astype(BF16), wukv_ref[...])
    kpe = kpe_ref[...]
    kpes = kpes_ref[...]
    kpe_sq = jnp.sum(kpe * kpe, axis=-1, keepdims=True)
    gq = gq_ref[...]
    gk = gk_ref[...]
    scale = np.float32(MLA_QK ** -0.5)
    for h in range(MLA_HEADS):
        nope = qall[:, 3 * h * LANES:(3 * h + 1) * LANES]
        rp = qall[:, (3 * h + 1) * LANES:(3 * h + 2) * LANES]
        rps = qall[:, (3 * h + 2) * LANES:(3 * h + 3) * LANES]
        ms = (jnp.sum(nope * nope, axis=-1, keepdims=True)
              + jnp.sum(rp * rp, axis=-1, keepdims=True)) * np.float32(1.0 / MLA_QK)
        r = lax.rsqrt(ms + RMS_EPS) * scale
        q_out[:, 2 * h * LANES:(2 * h + 1) * LANES] = (nope * r * gq[0:1]).astype(BF16)
        q_out[:, (2 * h + 1) * LANES:(2 * h + 2) * LANES] = (
            (rp * r * gq[1:2]) * cos + (rps * r * gq[2:3]) * sin).astype(BF16)
        knope = kvall[:, 2 * h * LANES:(2 * h + 1) * LANES]
        ms = (jnp.sum(knope * knope, axis=-1, keepdims=True) + kpe_sq) * np.float32(1.0 / MLA_QK)
        r = lax.rsqrt(ms + RMS_EPS)
        k_out[:, 2 * h * LANES:(2 * h + 1) * LANES] = (knope * r * gk[0:1]).astype(BF16)
        k_out[:, (2 * h + 1) * LANES:(2 * h + 2) * LANES] = (
            (kpe * r * gk[1:2]) * cos + (kpes * r * gk[2:3]) * sin).astype(BF16)
        v_out[:, h * LANES:(h + 1) * LANES] = kvall[:, (2 * h + 1) * LANES:(2 * h + 2) * LANES].astype(BF16)


def _mla_prep(z, cos, sin, cq_norm, ckv_norm, wuq_p, wukv, gq, gk, col0, tm=256):
    n = z.shape[0]
    hq = MLA_HEADS * 2 * LANES
    full = lambda a: pl.BlockSpec(a.shape, lambda i: (0,) * a.ndim)
    return pl.pallas_call(
        _mla_prep_body,
        grid=(n // tm,),
        in_specs=[pl.BlockSpec((tm, 2 * LANES), lambda i: (i, col0 // 2)),
                  pl.BlockSpec((tm, LANES), lambda i: (i, col0 + 2)),
                  pl.BlockSpec((tm, LANES), lambda i: (i, col0 + 3)),
                  pl.BlockSpec((tm, LANES), lambda i: (i, col0 + 4)),
                  pl.BlockSpec((tm, LANES), lambda i: (i, 0)),
                  pl.BlockSpec((tm, LANES), lambda i: (i, 0)),
                  full(cq_norm), full(ckv_norm), full(wuq_p), full(wukv), full(gq), full(gk)],
        out_specs=[pl.BlockSpec((tm, hq), lambda i: (i, 0)),
                   pl.BlockSpec((tm, hq), lambda i: (i, 0)),
                   pl.BlockSpec((tm, MLA_HEADS * LANES), lambda i: (i, 0))],
        out_shape=[jax.ShapeDtypeStruct((n, hq), BF16),
                   jax.ShapeDtypeStruct((n, hq), BF16),
                   jax.ShapeDtypeStruct((n, MLA_HEADS * LANES), BF16)],
        compiler_params=_cparams(("parallel",)),
    )(z, z, z, z, cos, sin, cq_norm, ckv_norm, wuq_p, wukv, gq, gk)


def _flash_step(q, k, v, m_ref, l_ref, acc_ref, mask, scale=None):
    s = _dot_nt(q, k)
    if scale is not None:
        s = s * scale
    if mask is not None:
        s = jnp.where(mask, s, NEG_BIG)
    m_prev = m_ref[...]
    m_new = jnp.maximum(m_prev, jnp.max(s, axis=-1, keepdims=True))
    alpha = jnp.exp(m_prev - m_new)
    p = jnp.exp(s - m_new)
    l_ref[...] = alpha * l_ref[...] + jnp.sum(p, axis=-1, keepdims=True)
    acc_ref[...] = alpha * acc_ref[...] + _dot(p.astype(BF16), v.astype(BF16))
    m_ref[...] = m_new


def _flash_init(m_ref, l_ref, acc_ref):
    m_ref[...] = jnp.full_like(m_ref, NEG_BIG)
    l_ref[...] = jnp.zeros_like(l_ref)
    acc_ref[...] = jnp.zeros_like(acc_ref)


def _mla_attn_body(q_ref, k_ref, v_ref, o_ref, m_ref, l_ref, acc_ref):
    qi = pl.program_id(2)
    ki = pl.program_id(3)

    @pl.when(ki == 0)
    def _():
        _flash_init(m_ref, l_ref, acc_ref)

    @pl.when(ki < qi)
    def _():
        _flash_step(q_ref[...], k_ref[...], v_ref[...], m_ref, l_ref, acc_ref, None)

    @pl.when(ki == qi)
    def _():
        t = q_ref.shape[0]
        row = lax.broadcasted_iota(jnp.int32, (t, t), 0)
        col = lax.broadcasted_iota(jnp.int32, (t, t), 1)
        _flash_step(q_ref[...], k_ref[...], v_ref[...], m_ref, l_ref, acc_ref, col <= row)
        o_ref[...] = acc_ref[...] / l_ref[...]


def _mla_attention(q, k, v, batch, seq, t=512):
    n = q.shape[0]
    nt = seq // t
    return pl.pallas_call(
        _mla_attn_body,
        grid=(batch, MLA_HEADS, nt, nt),
        in_specs=[pl.BlockSpec((t, 2 * LANES), lambda b, h, qi, ki: (b * nt + qi, h)),
                  pl.BlockSpec((t, 2 * LANES), lambda b, h, qi, ki: (b * nt + jnp.minimum(ki, qi), h)),
                  pl.BlockSpec((t, LANES), lambda b, h, qi, ki: (b * nt + jnp.minimum(ki, qi), h))],
        out_specs=pl.BlockSpec((t, LANES), lambda b, h, qi, ki: (b * nt + qi, h)),
        out_shape=jax.ShapeDtypeStruct((n, MLA_HEADS * LANES), F32),
        scratch_shapes=[pltpu.VMEM((t, 1), F32), pltpu.VMEM((t, 1), F32), pltpu.VMEM((t, LANES), F32)],
        compiler_params=_cparams(("parallel", "parallel", "parallel", "arbitrary")),
    )(q, k, v)


def _moba_norm_body(q_ref, k_ref, gq_ref, gk_ref, qn_ref, kn_ref, km_ref):
    qn_ref[...] = _rms(q_ref[...], gq_ref[...]).astype(BF16)
    kn = _rms(k_ref[...], gk_ref[...])
    kn_ref[...] = kn.astype(BF16)
    km_ref[0] = jnp.mean(kn, axis=0, keepdims=True)


def _moba_norm(z, gq, gk, col_q, col_k):
    n = z.shape[0]
    h = MOBA_HEADS
    nblk = n // MOBA_BLOCK
    blk = lambda c0: pl.BlockSpec((MOBA_BLOCK, LANES), lambda i, hh: (i, c0 + hh))
    vec = pl.BlockSpec((1, LANES), lambda i, hh: (0, 0))
    return pl.pallas_call(
        _moba_norm_body,
        grid=(nblk, h),
        in_specs=[blk(col_q), blk(col_k), vec, vec],
        out_specs=[blk(0), blk(0), pl.BlockSpec((1, 1, LANES), lambda i, hh: (i, 0, hh))],
        out_shape=[jax.ShapeDtypeStruct((n, h * LANES), BF16),
                   jax.ShapeDtypeStruct((n, h * LANES), BF16),
                   jax.ShapeDtypeStruct((nblk, 1, h * LANES), F32)],
        compiler_params=_cparams(("parallel", "parallel")),
    )(z, z, gq.reshape(1, -1), gk.reshape(1, -1))


def _moba_select_body(q_ref, km_ref, sel_ref, *, nb):
    j = pl.program_id(1)
    km = km_ref[0]
    if nb < LANES:
        km = jnp.concatenate([km, jnp.zeros((LANES - nb, km.shape[1]), F32)], axis=0)
    gate = _dot_nt(q_ref[...], km.astype(BF16))
    lane = lax.broadcasted_iota(jnp.int32, gate.shape, 1)
    past = lane < j
    masked = jnp.where(past, gate, -jnp.inf)
    s = masked
    kth = None
    for _ in range(MOBA_TOPK):
        kth = jnp.max(s, axis=-1, keepdims=True)
        s = jnp.where(s == kth, -jnp.inf, s)
    sel_ref[0] = jnp.where(past & (masked >= kth), 1.0, 0.0)


def _moba_select(qn, kmean, batch, nb):
    n = qn.shape[0]
    h = MOBA_HEADS
    return pl.pallas_call(
        functools.partial(_moba_select_body, nb=nb),
        grid=(batch, nb, h),
        in_specs=[pl.BlockSpec((MOBA_BLOCK, LANES), lambda b, j, hh: (b * nb + j, hh)),
                  pl.BlockSpec((1, nb, LANES), lambda b, j, hh: (b, 0, hh))],
        out_specs=pl.BlockSpec((1, MOBA_BLOCK, LANES), lambda b, j, hh: (hh, b * nb + j, 0)),
        out_shape=jax.ShapeDtypeStruct((h, n, LANES), F32),
        compiler_params=_cparams(("parallel", "parallel", "parallel")),
    )(qn, kmean)


def _moba_attn_body(q_ref, k_ref, v_ref, sel_ref, o_ref, m_ref, l_ref, acc_ref):
    j = pl.program_id(2)
    t = pl.program_id(3)
    scale = np.float32(LANES ** -0.5)

    @pl.when(t == 0)
    def _():
        _flash_init(m_ref, l_ref, acc_ref)
        nq = q_ref.shape[0]
        row = lax.broadcasted_iota(jnp.int32, (nq, nq), 0)
        col = lax.broadcasted_iota(jnp.int32, (nq, nq), 1)
        _flash_step(q_ref[...], k_ref[...], v_ref[...], m_ref, l_ref, acc_ref, col <= row, scale)

    @pl.when((t > 0) & (t <= j))
    def _():
        sel = sel_ref[0]
        lane = lax.broadcasted_iota(jnp.int32, sel.shape, 1)
        picked = jnp.sum(jnp.where(lane == j - t, sel, 0.0), axis=-1, keepdims=True) > 0.5
        _flash_step(q_ref[...], k_ref[...], v_ref[...], m_ref, l_ref, acc_ref, picked, scale)

    @pl.when(t == j)
    def _():
        o_ref[...] = acc_ref[...] / l_ref[...]


def _moba_attention(qn, kn, v, sel, batch, nb, col_v):
    n = qn.shape[0]
    h = MOBA_HEADS
    blk = MOBA_BLOCK
    kv_row = lambda b, hh, j, t: b * nb + jnp.maximum(j - t, 0)
    return pl.pallas_call(
        _moba_attn_body,
        grid=(batch, h, nb, nb),
        in_specs=[pl.BlockSpec((blk, LANES), lambda b, hh, j, t: (b * nb + j, hh)),
                  pl.BlockSpec((blk, LANES), lambda b, hh, j, t: (kv_row(b, hh, j, t), hh)),
                  pl.BlockSpec((blk, LANES), lambda b, hh, j, t: (kv_row(b, hh, j, t), col_v + hh)),
                  pl.BlockSpec((1, blk, LANES), lambda b, hh, j, t: (hh, b * nb + j, 0))],
        out_specs=pl.BlockSpec((blk, LANES), lambda b, hh, j, t: (b * nb + j, hh)),
        out_shape=jax.ShapeDtypeStruct((n, h * LANES), F32),
        scratch_shapes=[pltpu.VMEM((blk, 1), F32), pltpu.VMEM((blk, 1), F32), pltpu.VMEM((blk, LANES), F32)],
        compiler_params=_cparams(("parallel", "parallel", "parallel", "arbitrary")),
    )(qn, kn, v, sel)


_PEER_NTOP = PEER_TOPK + 1
_PEER_PAD = 24


def _top_values(s, count):
    rows = lax.broadcasted_iota(jnp.int32, (_PEER_PAD, s.shape[1]), 0)
    top = jnp.full((_PEER_PAD, s.shape[1]), -jnp.inf, F32)
    for a in range(count):
        m = jnp.max(s, axis=0, keepdims=True)
        top = jnp.where(rows == a, m, top)
        s = jnp.where(s == m, -jnp.inf, s)
    return top


def _peer_route_body(x_ref, g_ref, wq_ref, sk_ref, xnt_ref, thr_ref, s2_ref, e1_ref, e2_ref, q_scr):
    h = pl.program_id(1)

    @pl.when(h == 0)
    def _():
        xn = _rms(x_ref[...], g_ref[...])
        xnt_ref[...] = xn.T.astype(BF16)
        q = _dot(xn.astype(BF16), wq_ref[...])
        for hp in range(2 * PEER_HEADS):
            q_scr[hp] = q[:, hp * LANES:(hp + 1) * LANES].astype(BF16)

    s1 = _dot_nt(sk_ref[0, 0], q_scr[2 * h])
    s2 = _dot_nt(sk_ref[0, 1], q_scr[2 * h + 1])
    top1 = _top_values(s1, _PEER_NTOP)
    top2 = _top_values(s2, _PEER_NTOP)
    rows = lax.broadcasted_iota(jnp.int32, top1.shape, 0)
    top1_tail = jnp.where(rows >= 4, top1, -jnp.inf)
    cand = jnp.concatenate([top1[a:a + 1] + top2 for a in range(4)]
                           + [top2[b:b + 1] + top1_tail for b in range(3)], axis=0)
    s = cand
    kth = nxt = None
    for a in range(_PEER_NTOP):
        kth, nxt = nxt, jnp.max(s, axis=0, keepdims=True)
        s = jnp.where(s == nxt, -jnp.inf, s)
    tau = 0.5 * (kth + nxt)
    m1 = top1[0:1]
    m2 = top2[0:1]
    z = jnp.sum(jnp.where(cand >= tau, jnp.exp(cand - (m1 + m2)), 0.0), axis=0, keepdims=True)
    thr_ref[0] = tau - s1
    s2_ref[0] = s2
    e1_ref[0] = jnp.exp(s1 - m1) / z
    e2_ref[0] = jnp.exp(s2 - m2)


def _peer_route(x2d, gain, wq_bf16, sk_bf16, t=256):
    n, d = x2d.shape
    hq = wq_bf16.shape[1]
    routed = jax.ShapeDtypeStruct((PEER_HEADS, PEER_KEYS, n), F32)
    rspec = pl.BlockSpec((1, PEER_KEYS, t), lambda i, h: (h, 0, i))
    return pl.pallas_call(
        _peer_route_body,
        grid=(n // t, PEER_HEADS),
        in_specs=[pl.BlockSpec((t, d), lambda i, h: (i, 0)),
                  pl.BlockSpec((1, d), lambda i, h: (0, 0)),
                  pl.BlockSpec((d, hq), lambda i, h: (0, 0)),
                  pl.BlockSpec((1, 2, PEER_KEYS, LANES), lambda i, h: (h, 0, 0, 0))],
        out_specs=[pl.BlockSpec((d, t), lambda i, h: (0, i)), rspec, rspec, rspec, rspec],
        out_shape=[jax.ShapeDtypeStruct((d, n), BF16), routed, routed, routed, routed],
        scratch_shapes=[pltpu.VMEM((hq // LANES, t, LANES), BF16)],
        compiler_params=_cparams(("parallel", "arbitrary")),
    )(x2d, gain.reshape(1, d), wq_bf16, sk_bf16)


def _peer_main_body(x_ref, xnt_ref, thr_ref, s2_ref, e1_ref, e2_ref, down_ref, upt_ref, o_ref,
                    acc_ref, p_ref, *, bi):
    e = pl.program_id(1)

    @pl.when(e == 0)
    def _():
        acc_ref[...] = jnp.zeros_like(acc_ref)

    act = _gelu(_dot(down_ref[...], xnt_ref[...]))
    for ii in range(bi):
        i = e * bi + ii
        w = jnp.zeros((PEER_KEYS, act.shape[1]), F32)
        for h in range(PEER_HEADS):
            thr = thr_ref[h, pl.ds(i, 1), :]
            e1 = e1_ref[h, pl.ds(i, 1), :]
            w = w + jnp.where(s2_ref[h] >= thr, e2_ref[h], 0.0) * e1
        p_ref[ii * PEER_KEYS:(ii + 1) * PEER_KEYS, :] = (
            w * act[ii * PEER_KEYS:(ii + 1) * PEER_KEYS]).astype(BF16)
    acc_ref[...] += _dot(upt_ref[...], p_ref[...])

    @pl.when(e == pl.num_programs(1) - 1)
    def _():
        o_ref[...] = x_ref[...] + acc_ref[...].T


def _peer_main(x2d, xnt, thr, s2, e1, e2, down_bf16, upt_bf16, t=512, bi=8):
    n, d = x2d.shape
    ne = down_bf16.shape[0]
    be = bi * PEER_KEYS
    rspec = pl.BlockSpec((PEER_HEADS, PEER_KEYS, t), lambda i, e: (0, 0, i))
    return pl.pallas_call(
        functools.partial(_peer_main_body, bi=bi),
        grid=(n // t, ne // be),
        in_specs=[pl.BlockSpec((t, d), lambda i, e: (i, 0)),
                  pl.BlockSpec((d, t), lambda i, e: (0, i)),
                  rspec, rspec, rspec, rspec,
                  pl.BlockSpec((be, d), lambda i, e: (e, 0)),
                  pl.BlockSpec((d, be), lambda i, e: (0, e))],
        out_specs=pl.BlockSpec((t, d), lambda i, e: (i, 0)),
        out_shape=jax.ShapeDtypeStruct((n, d), F32),
        scratch_shapes=[pltpu.VMEM((d, t), F32), pltpu.VMEM((be, t), BF16)],
        compiler_params=_cparams(("parallel", "arbitrary")),
    )(x2d, xnt, thr, s2, e1, e2, down_bf16, upt_bf16)


def _peer_ffn(x2d, gain, w_query, sub_keys, expert_down, expert_up):
    xnt, thr, s2, e1, e2 = _peer_route(x2d, gain, w_query.astype(BF16), sub_keys.astype(BF16))
    return _peer_main(x2d, xnt, thr, s2, e1, e2, expert_down.astype(BF16), expert_up.T.astype(BF16))


def _pad_cols(w, width):
    return jnp.pad(w, ((0, 0), (0, width - w.shape[1])))


def _swap_halves(w):
    half = w.shape[-1] // 2
    return jnp.concatenate([w[..., half:], w[..., :half]], axis=-1)


def _rope_gains(g):
    rope = g[MLA_NOPE:]
    return jnp.stack([g[:MLA_NOPE],
                      jnp.pad(rope, (0, LANES - MLA_ROPE)),
                      jnp.pad(_swap_halves(rope), (0, LANES - MLA_ROPE))])


def kernel(x, positions, lb_logits, l0_norm_mix, l0_w_in, l0_hgrn_out_norm, l0_gmlp_v_norm, l0_gmlp_w_s, l0_gmlp_b_s, l0_w_out, l0_norm_ffn, l0_peer_w_query, l0_peer_sub_keys, l0_peer_expert_down, l0_peer_expert_up, l1_norm_mix, l1_w_in, l1_mla_cq_norm, l1_mla_ckv_norm, l1_mla_w_uq, l1_mla_w_ukv, l1_mla_q_norm, l1_mla_k_norm, l1_moba_q_norm, l1_moba_k_norm, l1_w_out, l1_norm_ffn, l1_peer_w_query, l1_peer_sub_keys, l1_peer_expert_down, l1_peer_expert_up):
    batch, seq, d = x.shape
    n = batch * seq
    x2d = x.reshape(n, d)

    lb0 = jax.nn.softmax(lb_logits.astype(F32), axis=0)[0]
    z0 = _norm_matmul(x2d, l0_norm_mix, l0_w_in.astype(BF16))
    out_a, out_b = _l0_mixer(z0, lb0, l0_hgrn_out_norm, l0_gmlp_v_norm, l0_gmlp_w_s, l0_gmlp_b_s, batch, seq)
    half = out_a.shape[1]
    w_out0 = l0_w_out.astype(BF16)
    x2d = _out_proj(x2d, out_a, out_b, w_out0[:half], w_out0[half:])
    x2d = _peer_ffn(x2d, l0_norm_ffn, l0_peer_w_query, l0_peer_sub_keys, l0_peer_expert_down, l0_peer_expert_up)

    q_lora, kv_lora = l1_mla_cq_norm.shape[0], l1_mla_ckv_norm.shape[0]
    c0 = q_lora + kv_lora
    dw = MOBA_HEADS * LANES
    w1 = l1_w_in
    k_pe_w = w1[:, c0:c0 + MLA_ROPE]
    moba_w = w1[:, c0 + MLA_ROPE:]
    w1p = jnp.concatenate([moba_w, w1[:, :c0], _pad_cols(k_pe_w, LANES),
                           _pad_cols(_swap_halves(k_pe_w), LANES)], axis=1).astype(BF16)
    z1 = _norm_matmul(x2d, l1_norm_mix, w1p)

    half_r = MLA_ROPE // 2
    inv_freq = ROPE_THETA ** (-jnp.arange(half_r, dtype=F32) / half_r)
    ang = positions.reshape(n, 1).astype(F32) * inv_freq
    zeros = jnp.zeros((n, LANES - MLA_ROPE), F32)
    cos_t = jnp.concatenate([jnp.cos(ang), jnp.cos(ang), zeros], axis=1)
    sin_t = jnp.concatenate([-jnp.sin(ang), jnp.sin(ang), zeros], axis=1)

    wuq = l1_mla_w_uq.reshape(q_lora, MLA_HEADS, MLA_QK)
    wuq_p = jnp.concatenate(
        [wuq[..., :MLA_NOPE],
         jnp.pad(wuq[..., MLA_NOPE:], ((0, 0), (0, 0), (0, LANES - MLA_ROPE))),
         jnp.pad(_swap_halves(wuq[..., MLA_NOPE:]), ((0, 0), (0, 0), (0, LANES - MLA_ROPE)))],
        axis=-1).reshape(q_lora, MLA_HEADS * 3 * LANES).astype(BF16)
    q_c, k_c, v_c = _mla_prep(z1, cos_t, sin_t, l1_mla_cq_norm.reshape(1, -1), l1_mla_ckv_norm.reshape(1, -1),
                              wuq_p, l1_mla_w_ukv.astype(BF16), _rope_gains(l1_mla_q_norm),
                              _rope_gains(l1_mla_k_norm), col0=3 * MOBA_HEADS)
    out_c = _mla_attention(q_c, k_c, v_c, batch, seq)

    nb = seq // MOBA_BLOCK
    qn, kn, kmean = _moba_norm(z1, l1_moba_q_norm, l1_moba_k_norm, col_q=0, col_k=MOBA_HEADS)
    sel = _moba_select(qn, kmean.reshape(batch, nb, dw), batch, nb)
    out_d = _moba_attention(qn, kn, z1, sel, batch, nb, col_v=2 * MOBA_HEADS)

    w_out1 = l1_w_out.astype(BF16)
    x2d = _out_proj(x2d, out_c, out_d, w_out1[:out_c.shape[1]], w_out1[out_c.shape[1]:])
    x2d = _peer_ffn(x2d, l1_norm_ffn, l1_peer_w_query, l1_peer_sub_keys, l1_peer_expert_down, l1_peer_expert_up)
    return x2d.reshape(batch, seq, d)
```

```python
import functools

import numpy as np
import jax
import jax.numpy as jnp
from jax import lax
from jax.experimental import pallas as pl
from jax.experimental.pallas import tpu as pltpu

F32 = jnp.float32
BF16 = jnp.bfloat16

RMS_EPS = 1e-6
LANES = 128
VMEM_LIMIT = 56 * 1024 * 1024

HGRN_HEADS = 4
GMLP_GROUPS = 4
MIX_CHUNK = 128
MLA_HEADS = 4
MLA_NOPE = 128
MLA_ROPE = 64
MLA_QK = MLA_NOPE + MLA_ROPE
ROPE_THETA = 10000.0
MOBA_HEADS = 4
MOBA_BLOCK = 256
MOBA_TOPK = 3
PEER_HEADS = 8
PEER_KEYS = 128
PEER_TOPK = 16
NEG_BIG = -1e30


def _cparams(semantics):
    return pltpu.CompilerParams(dimension_semantics=semantics, vmem_limit_bytes=VMEM_LIMIT)


def _dot(a, b):
    return jnp.dot(a, b, preferred_element_type=F32)


def _dot_nt(a, b):
    return lax.dot_general(a, b, (((1,), (1,)), ((), ())), preferred_element_type=F32)


def _dot_tn(a, b):
    return lax.dot_general(a, b, (((0,), (0,)), ((), ())), preferred_element_type=F32)


def _rms(x, gain):
    return x * lax.rsqrt(jnp.mean(x * x, axis=-1, keepdims=True) + RMS_EPS) * gain


def _gelu(x):
    c = np.float32(np.sqrt(2.0 / np.pi))
    half = 0.5 * x
    return half + half * jnp.tanh(x * (c + np.float32(c * 0.044715) * (x * x)))


def _silu(x):
    return x * jax.nn.sigmoid(x)


def _norm_matmul_body(x_ref, g_ref, w_ref, o_ref):
    y = _rms(x_ref[...], g_ref[...])
    o_ref[...] = _dot(y.astype(BF16), w_ref[...])


def _norm_matmul(x2d, gain, w_bf16, tm=512):
    n, d = x2d.shape
    m = w_bf16.shape[1]
    return pl.pallas_call(
        _norm_matmul_body,
        grid=(n // tm,),
        in_specs=[pl.BlockSpec((tm, d), lambda i: (i, 0)),
                  pl.BlockSpec((1, d), lambda i: (0, 0)),
                  pl.BlockSpec((d, m), lambda i: (0, 0))],
        out_specs=pl.BlockSpec((tm, m), lambda i: (i, 0)),
        out_shape=jax.ShapeDtypeStruct((n, m), F32),
        compiler_params=_cparams(("parallel",)),
    )(x2d, gain.reshape(1, d), w_bf16)


def _out_proj_body(x_ref, a_ref, b_ref, wa_ref, wb_ref, o_ref):
    o_ref[...] = (x_ref[...] + _dot(a_ref[...].astype(BF16), wa_ref[...])
                  + _dot(b_ref[...].astype(BF16), wb_ref[...]))


def _out_proj(x2d, a, b, wa_bf16, wb_bf16, tm=512):
    n, d = x2d.shape
    ka, kb = a.shape[1], b.shape[1]
    return pl.pallas_call(
        _out_proj_body,
        grid=(n // tm,),
        in_specs=[pl.BlockSpec((tm, d), lambda i: (i, 0)),
                  pl.BlockSpec((tm, ka), lambda i: (i, 0)),
                  pl.BlockSpec((tm, kb), lambda i: (i, 0)),
                  pl.BlockSpec((ka, d), lambda i: (0, 0)),
                  pl.BlockSpec((kb, d), lambda i: (0, 0))],
        out_specs=pl.BlockSpec((tm, d), lambda i: (i, 0)),
        out_shape=jax.ShapeDtypeStruct((n, d), F32),
        compiler_params=_cparams(("parallel",)),
    )(x2d, a, b, wa_bf16, wb_bf16)


_HGRN_LEVELS = (64, 32, 16, 8, 4, 2, 1)


def _hgrn_exponent_matrix(c):
    a = np.zeros((len(_HGRN_LEVELS) + 1, c, c), np.float32)
    r = np.arange(c)[:, None]
    u = np.arange(c)[None, :]
    a[0] = (u <= r)
    for li, w in enumerate(_HGRN_LEVELS):
        p = (r // (2 * w)) * (2 * w) + w
        second = r >= p
        a[li + 1] = np.where(second, (u >= p) & (u <= r), (u > r) & (u <= p - 1))
    return a.reshape(-1, c)


def _l0_mix_body(q_ref, fg_ref, inp_ref, og_ref, u_ref, v_ref, lb_ref, on_ref, vn_ref,
                 ws_ref, bs_ref, amat_ref, oa_ref, ob_ref, state_ref):
    c = MIX_CHUNK

    @pl.when(pl.program_id(2) == 0)
    def _():
        state_ref[...] = jnp.zeros_like(state_ref)

    row = lax.broadcasted_iota(jnp.int32, (c, c), 0)
    col = lax.broadcasted_iota(jnp.int32, (c, c), 1)
    rowc = lax.broadcasted_iota(jnp.int32, (c, 1), 0)

    q = q_ref[...]
    lb = lb_ref[...]
    f = lb + (1.0 - lb) * jax.nn.sigmoid(fg_ref[...])
    k = 1.0 - f
    g = jnp.log(f)
    val = _silu(inp_ref[...])
    g1 = g.astype(BF16)
    r1 = g - g1.astype(F32)
    g2 = r1.astype(BF16)
    g3 = (r1 - g2.astype(F32)).astype(BF16)
    amat = amat_ref[...]
    dall = _dot(amat, g1) + _dot(amat, g2) + _dot(amat, g3)
    b = dall[0:c]

    scores = jnp.where(row == col, _dot_nt(q.astype(BF16), k.astype(BF16)), 0.0)
    for li, w in enumerate(_HGRN_LEVELS):
        e = jnp.exp(dall[(li + 1) * c:(li + 2) * c])
        sh = int(np.log2(w))
        second = (jnp.right_shift(rowc, sh) & 1) == 1
        qt = jnp.where(second, q * e, 0.0).astype(BF16)
        kt = jnp.where(second, 0.0, k * e).astype(BF16)
        same = jnp.right_shift(row, sh + 1) == jnp.right_shift(col, sh + 1)
        scores = scores + jnp.where(same, _dot_nt(qt, kt), 0.0)
    val16 = val.astype(BF16)
    o = _dot(scores.astype(BF16), val16)
    state_t = state_ref[...]
    o = o + _dot_nt((q * jnp.exp(b)).astype(BF16), state_t.astype(BF16))
    b_last = b[c - 1:c]
    k_dec = (k * jnp.exp(b_last - b)).astype(BF16)
    state_ref[...] = state_t * jnp.exp(b_last) + _dot_tn(val16, k_dec)
    oa_ref[...] = _rms(o, on_ref[...]) * _silu(og_ref[...])

    u = _gelu(u_ref[...])
    vn = _rms(_gelu(v_ref[...]), vn_ref[...])
    w_causal = jnp.where(row >= col, ws_ref[0], 0.0).astype(BF16)
    sv = _dot(w_causal, vn.astype(BF16)) + bs_ref[0]
    ob_ref[...] = u * sv


def _l0_mixer(z, lb, out_norm, v_norm, w_s, b_s, batch, seq):
    n = z.shape[0]
    c = MIX_CHUNK
    nc = seq // c
    h = HGRN_HEADS
    amat = jnp.asarray(_hgrn_exponent_matrix(c), BF16)
    na = amat.shape[0]

    def zcol(group):
        return pl.BlockSpec((c, LANES), lambda b, hh, i, g=group: (b * nc + i, g * h + hh))

    def vec():
        return pl.BlockSpec((1, LANES), lambda b, hh, i: (0, hh))

    out_spec = pl.BlockSpec((c, LANES), lambda b, hh, i: (b * nc + i, hh))
    return pl.pallas_call(
        _l0_mix_body,
        grid=(batch, h, nc),
        in_specs=[zcol(0), zcol(1), zcol(2), zcol(3), zcol(4), zcol(5),
                  vec(), vec(), vec(),
                  pl.BlockSpec((1, c, c), lambda b, hh, i: (hh, 0, 0)),
                  pl.BlockSpec((1, c, 1), lambda b, hh, i: (hh, 0, 0)),
                  pl.BlockSpec((na, c), lambda b, hh, i: (0, 0))],
        out_specs=[out_spec, out_spec],
        out_shape=[jax.ShapeDtypeStruct((n, h * LANES), F32),
                   jax.ShapeDtypeStruct((n, h * LANES), F32)],
        scratch_shapes=[pltpu.VMEM((LANES, LANES), F32)],
        compiler_params=_cparams(("parallel", "parallel", "arbitrary")),
    )(z, z, z, z, z, z, lb.reshape(1, -1), out_norm.reshape(1, -1), v_norm.reshape(1, -1),
      w_s, b_s.reshape(GMLP_GROUPS, c, 1), amat)


def _mla_prep_body(cq_ref, ckv_ref, kpe_ref, kpes_ref, cos_ref, sin_ref, cqn_ref, ckvn_ref,
                   wuq_ref, wukv_ref, gq_ref, gk_ref, q_out, k_out, v_out):
    cos = cos_ref[...]
    sin = sin_ref[...]
    qn = _rms(cq_ref[...], cqn_ref[...])
    kvn = _rms(ckv_ref[...], ckvn_ref[...])
    qall = _dot(qn.astype(BF16), wuq_ref[...])
    kvall = _dot(kvn.astype(BF16), wukv_ref[...])
    kpe = kpe_ref[...]
    kpes = kpes_ref[...]
    kpe_sq = jnp.sum(kpe * kpe, axis=-1, keepdims=True)
    gq = gq_ref[...]
    gk = gk_ref[...]
    scale = np.float32(MLA_QK ** -0.5)
    for h in range(MLA_HEADS):
        nope = qall[:, 3 * h * LANES:(3 * h + 1) * LANES]
        rp = qall[:, (3 * h + 1) * LANES:(3 * h + 2) * LANES]
        rps = qall[:, (3 * h + 2) * LANES:(3 * h + 3) * LANES]
        ms = (jnp.sum(nope * nope, axis=-1, keepdims=True)
              + jnp.sum(rp * rp, axis=-1, keepdims=True)) * np.float32(1.0 / MLA_QK)
        r = lax.rsqrt(ms + RMS_EPS) * scale
        q_out[:, 2 * h * LANES:(2 * h + 1) * LANES] = (nope * r * gq[0:1]).astype(BF16)
        q_out[:, (2 * h + 1) * LANES:(2 * h + 2) * LANES] = (
            (rp * r * gq[1:2]) * cos + (rps * r * gq[2:3]) * sin).astype(BF16)
        knope = kvall[:, 2 * h * LANES:(2 * h + 1) * LANES]
        ms = (jnp.sum(knope * knope, axis=-1, keepdims=True) + kpe_sq) * np.float32(1.0 / MLA_QK)
        r = lax.rsqrt(ms + RMS_EPS)
        k_out[:, 2 * h * LANES:(2 * h + 1) * LANES] = (knope * r * gk[0:1]).astype(BF16)
        k_out[:, (2 * h + 1) * LANES:(2 * h + 2) * LANES] = (
            (kpe * r * gk[1:2]) * cos + (kpes * r * gk[2:3]) * sin).astype(BF16)
        v_out[:, h * LANES:(h + 1) * LANES] = kvall[:, (2 * h + 1) * LANES:(2 * h + 2) * LANES].astype(BF16)


def _mla_prep(z, cos, sin, cq_norm, ckv_norm, wuq_p, wukv, gq, gk, col0, tm=256):
    n = z.shape[0]
    hq = MLA_HEADS * 2 * LANES
    full = lambda a: pl.BlockSpec(a.shape, lambda i: (0,) * a.ndim)
    return pl.pallas_call(
        _mla_prep_body,
        grid=(n // tm,),
        in_specs=[pl.BlockSpec((tm, 2 * LANES), lambda i: (i, col0 // 2)),
                  pl.BlockSpec((tm, LANES), lambda i: (i, col0 + 2)),
                  pl.BlockSpec((tm, LANES), lambda i: (i, col0 + 3)),
                  pl.BlockSpec((tm, LANES), lambda i: (i, col0 + 4)),
                  pl.BlockSpec((tm, LANES), lambda i: (i, 0)),
                  pl.BlockSpec((tm, LANES), lambda i: (i, 0)),
                  full(cq_norm), full(ckv_norm), full(wuq_p), full(wukv), full(gq), full(gk)],
        out_specs=[pl.BlockSpec((tm, hq), lambda i: (i, 0)),
                   pl.BlockSpec((tm, hq), lambda i: (i, 0)),
                   pl.BlockSpec((tm, MLA_HEADS * LANES), lambda i: (i, 0))],
        out_shape=[jax.ShapeDtypeStruct((n, hq), BF16),
                   jax.ShapeDtypeStruct((n, hq), BF16),
                   jax.ShapeDtypeStruct((n, MLA_HEADS * LANES), BF16)],
        compiler_params=_cparams(("parallel",)),
    )(z, z, z, z, cos, sin, cq_norm, ckv_norm, wuq_p, wukv, gq, gk)


def _flash_update(carry, q, tiles):
    m_prev, l_prev, acc = carry
    scores = [jnp.where(mask, _dot_nt(q, k), NEG_BIG) for k, _, mask in tiles]
    m_new = m_prev
    for s in scores:
        m_new = jnp.maximum(m_new, jnp.max(s, axis=-1, keepdims=True))
    alpha = jnp.exp(m_prev - m_new)
    l_new = alpha * l_prev
    acc = alpha * acc
    for s, (_, v, _) in zip(scores, tiles):
        p = jnp.exp(s - m_new)
        l_new = l_new + jnp.sum(p, axis=-1, keepdims=True)
        acc = acc + _dot(p.astype(BF16), v)
    return m_new, l_new, acc


def _flash_start(tq, dv):
    return (jnp.full((tq, 1), NEG_BIG, F32), jnp.zeros((tq, 1), F32), jnp.zeros((tq, dv), F32))


_MLA_GROUP = 1


def _mla_attn_body(q_ref, k_ref, v_ref, o_ref):
    qi = pl.program_id(2)
    q = q_ref[...]
    t = q.shape[0]
    row = lax.broadcasted_iota(jnp.int32, (t, t), 0)
    col = lax.broadcasted_iota(jnp.int32, (t, t), 1)
    ahead = col - row
    groups = (qi + _MLA_GROUP) // _MLA_GROUP

    def step(it, carry):
        tiles = []
        for c in range(_MLA_GROUP):
            n = (groups - 1 - it) * _MLA_GROUP + c
            start = pl.multiple_of(n * t, t)
            tiles.append((k_ref[pl.ds(start, t), :], v_ref[pl.ds(start, t), :], ahead <= (qi - n) * t))
        return _flash_update(carry, q, tiles)

    carry = lax.fori_loop(0, groups, step, _flash_start(t, v_ref.shape[1]))
    o_ref[...] = carry[2] / carry[1]


def _mla_attention(q, k, v, batch, seq, t=512):
    n = q.shape[0]
    nt = seq // t
    return pl.pallas_call(
        _mla_attn_body,
        grid=(batch, MLA_HEADS, nt),
        in_specs=[pl.BlockSpec((t, 2 * LANES), lambda b, h, qi: (b * nt + qi, h)),
                  pl.BlockSpec((seq, 2 * LANES), lambda b, h, qi: (b, h)),
                  pl.BlockSpec((seq, LANES), lambda b, h, qi: (b, h))],
        out_specs=pl.BlockSpec((t, LANES), lambda b, h, qi: (b * nt + qi, h)),
        out_shape=jax.ShapeDtypeStruct((n, MLA_HEADS * LANES), F32),
        compiler_params=_cparams(("parallel", "parallel", "arbitrary")),
    )(q, k, v)


def _moba_norm_body(q_ref, k_ref, v_ref, gq_ref, gk_ref, qn_ref, kn_ref, vb_ref, km_ref):
    qn_ref[...] = (_rms(q_ref[...], gq_ref[...]) * np.float32(LANES ** -0.5)).astype(BF16)
    kn = _rms(k_ref[...], gk_ref[...])
    kn_ref[...] = kn.astype(BF16)
    vb_ref[...] = v_ref[...].astype(BF16)
    km_ref[0] = jnp.mean(kn, axis=0, keepdims=True)


def _moba_norm(z, gq, gk, col_q, col_k, col_v):
    n = z.shape[0]
    h = MOBA_HEADS
    nblk = n // MOBA_BLOCK
    blk = lambda c0: pl.BlockSpec((MOBA_BLOCK, LANES), lambda i, hh: (i, c0 + hh))
    vec = pl.BlockSpec((1, LANES), lambda i, hh: (0, 0))
    act = jax.ShapeDtypeStruct((n, h * LANES), BF16)
    return pl.pallas_call(
        _moba_norm_body,
        grid=(nblk, h),
        in_specs=[blk(col_q), blk(col_k), blk(col_v), vec, vec],
        out_specs=[blk(0), blk(0), blk(0), pl.BlockSpec((1, 1, LANES), lambda i, hh: (i, 0, hh))],
        out_shape=[act, act, act, jax.ShapeDtypeStruct((nblk, 1, h * LANES), F32)],
        compiler_params=_cparams(("parallel", "parallel")),
    )(z, z, z, gq.reshape(1, -1), gk.reshape(1, -1))


_MOBA_GROUP = 4


def _moba_attn_body(q_ref, k_ref, v_ref, km_ref, o_ref, *, nb):
    j = pl.program_id(2)
    q = q_ref[...]
    t = q.shape[0]

    km = km_ref[0]
    if nb < LANES:
        km = jnp.concatenate([km, jnp.zeros((LANES - nb, km.shape[1]), F32)], axis=0)
    gate = _dot_nt(q, km.astype(BF16))
    lane = lax.broadcasted_iota(jnp.int32, gate.shape, 1)
    past = lane < j
    masked = jnp.where(past, gate, -jnp.inf)
    s = masked
    kth = None
    for _ in range(MOBA_TOPK):
        kth = jnp.max(s, axis=-1, keepdims=True)
        s = jnp.where(s == kth, -jnp.inf, s)
    sel = jnp.where((past & (masked >= kth)) | (lane == j), 1.0, 0.0)

    row = lax.broadcasted_iota(jnp.int32, (t, t), 0)
    col = lax.broadcasted_iota(jnp.int32, (t, t), 1)
    ahead = col - row
    groups = (j + _MOBA_GROUP) // _MOBA_GROUP

    def step(it, carry):
        tiles = []
        for c in range(_MOBA_GROUP):
            n = (groups - 1 - it) * _MOBA_GROUP + c
            start = pl.multiple_of(n * t, t)
            picked = jnp.sum(jnp.where(lane == n, sel, 0.0), axis=-1, keepdims=True) > 0.5
            mask = picked & (ahead <= jnp.where(n == j, 0, t))
            tiles.append((k_ref[pl.ds(start, t), :], v_ref[pl.ds(start, t), :], mask))
        return _flash_update(carry, q, tiles)

    carry = lax.fori_loop(0, groups, step, _flash_start(t, v_ref.shape[1]))
    o_ref[...] = carry[2] / carry[1]


def _moba_attention(qn, kn, vb, kmean, batch, nb):
    n = qn.shape[0]
    h = MOBA_HEADS
    blk = MOBA_BLOCK
    seq = nb * blk
    return pl.pallas_call(
        functools.partial(_moba_attn_body, nb=nb),
        grid=(batch, h, nb),
        in_specs=[pl.BlockSpec((blk, LANES), lambda b, hh, j: (b * nb + j, hh)),
                  pl.BlockSpec((seq, LANES), lambda b, hh, j: (b, hh)),
                  pl.BlockSpec((seq, LANES), lambda b, hh, j: (b, hh)),
                  pl.BlockSpec((1, nb, LANES), lambda b, hh, j: (b, 0, hh))],
        out_specs=pl.BlockSpec((blk, LANES), lambda b, hh, j: (b * nb + j, hh)),
        out_shape=jax.ShapeDtypeStruct((n, h * LANES), F32),
        compiler_params=_cparams(("parallel", "parallel", "arbitrary")),
    )(qn, kn, vb, kmean)


_PEER_NTOP = PEER_TOPK + 1
_PEER_PAD = 24


def _top_values(s, count):
    rows = lax.broadcasted_iota(jnp.int32, (_PEER_PAD, s.shape[1]), 0)
    top = jnp.full((_PEER_PAD, s.shape[1]), -jnp.inf, F32)
    for a in range(count):
        m = jnp.max(s, axis=0, keepdims=True)
        top = jnp.where(rows == a, m, top)
        s = jnp.where(s == m, -jnp.inf, s)
    return top


def _peer_route_body(x_ref, g_ref, wq_ref, sk_ref, xnt_ref, thr_ref, s2_ref, e1_ref, e2_ref, q_scr):
    h = pl.program_id(1)

    @pl.when(h == 0)
    def _():
        xn = _rms(x_ref[...], g_ref[...])
        xnt_ref[...] = xn.T.astype(BF16)
        q = _dot(xn.astype(BF16), wq_ref[...])
        for hp in range(2 * PEER_HEADS):
            q_scr[hp] = q[:, hp * LANES:(hp + 1) * LANES].astype(BF16)

    s1 = _dot_nt(sk_ref[0, 0], q_scr[2 * h])
    s2 = _dot_nt(sk_ref[0, 1], q_scr[2 * h + 1])
    top1 = _top_values(s1, _PEER_NTOP)
    top2 = _top_values(s2, _PEER_NTOP)
    rows = lax.broadcasted_iota(jnp.int32, top1.shape, 0)
    top1_tail = jnp.where(rows >= 4, top1, -jnp.inf)
    cand = jnp.concatenate([top1[a:a + 1] + top2 for a in range(4)]
                           + [top2[b:b + 1] + top1_tail for b in range(3)], axis=0)
    s = cand
    kth = nxt = None
    for a in range(_PEER_NTOP):
        kth, nxt = nxt, jnp.max(s, axis=0, keepdims=True)
        s = jnp.where(s == nxt, -jnp.inf, s)
    tau = 0.5 * (kth + nxt)
    m1 = top1[0:1]
    m2 = top2[0:1]
    z = jnp.sum(jnp.where(cand >= tau, jnp.exp(cand - (m1 + m2)), 0.0), axis=0, keepdims=True)
    thr_ref[0] = tau - s1
    s2_ref[0] = s2
    e1_ref[0] = jnp.exp(s1 - m1) / z
    e2_ref[0] = jnp.exp(s2 - m2)


def _peer_route(x2d, gain, wq_bf16, sk_bf16, t=256):
    n, d = x2d.shape
    hq = wq_bf16.shape[1]
    routed = jax.ShapeDtypeStruct((PEER_HEADS, PEER_KEYS, n), F32)
    rspec = pl.BlockSpec((1, PEER_KEYS, t), lambda i, h: (h, 0, i))
    return pl.pallas_call(
        _peer_route_body,
        grid=(n // t, PEER_HEADS),
        in_specs=[pl.BlockSpec((t, d), lambda i, h: (i, 0)),
                  pl.BlockSpec((1, d), lambda i, h: (0, 0)),
                  pl.BlockSpec((d, hq), lambda i, h: (0, 0)),
                  pl.BlockSpec((1, 2, PEER_KEYS, LANES), lambda i, h: (h, 0, 0, 0))],
        out_specs=[pl.BlockSpec((d, t), lambda i, h: (0, i)), rspec, rspec, rspec, rspec],
        out_shape=[jax.ShapeDtypeStruct((d, n), BF16), routed, routed, routed, routed],
        scratch_shapes=[pltpu.VMEM((hq // LANES, t, LANES), BF16)],
        compiler_params=_cparams(("parallel", "arbitrary")),
    )(x2d, gain.reshape(1, d), wq_bf16, sk_bf16)


def _peer_main_body(x_ref, xnt_ref, thr_ref, s2_ref, e1_ref, e2_ref, down_ref, upt_ref, o_ref,
                    acc_ref, p_ref, *, bi):
    e = pl.program_id(1)

    @pl.when(e == 0)
    def _():
        acc_ref[...] = jnp.zeros_like(acc_ref)

    act = _gelu(_dot(down_ref[...], xnt_ref[...]))
    for ii in range(bi):
        i = e * bi + ii
        w = jnp.zeros((PEER_KEYS, act.shape[1]), F32)
        for h in range(PEER_HEADS):
            thr = thr_ref[h, pl.ds(i, 1), :]
            e1 = e1_ref[h, pl.ds(i, 1), :]
            w = w + jnp.where(s2_ref[h] >= thr, e2_ref[h], 0.0) * e1
        p_ref[ii * PEER_KEYS:(ii + 1) * PEER_KEYS, :] = (
            w * act[ii * PEER_KEYS:(ii + 1) * PEER_KEYS]).astype(BF16)
    acc_ref[...] += _dot(upt_ref[...], p_ref[...])

    @pl.when(e == pl.num_programs(1) - 1)
    def _():
        o_ref[...] = x_ref[...] + acc_ref[...].T


def _peer_main(x2d, xnt, thr, s2, e1, e2, down_bf16, upt_bf16, t=512, bi=8):
    n, d = x2d.shape
    ne = down_bf16.shape[0]
    be = bi * PEER_KEYS
    rspec = pl.BlockSpec((PEER_HEADS, PEER_KEYS, t), lambda i, e: (0, 0, i))
    return pl.pallas_call(
        functools.partial(_peer_main_body, bi=bi),
        grid=(n // t, ne // be),
        in_specs=[pl.BlockSpec((t, d), lambda i, e: (i, 0)),
                  pl.BlockSpec((d, t), lambda i, e: (0, i)),
                  rspec, rspec, rspec, rspec,
                  pl.BlockSpec((be, d), lambda i, e: (e, 0)),
                  pl.BlockSpec((d, be), lambda i, e: (0, e))],
        out_specs=pl.BlockSpec((t, d), lambda i, e: (i, 0)),
        out_shape=jax.ShapeDtypeStruct((n, d), F32),
        scratch_shapes=[pltpu.VMEM((d, t), F32), pltpu.VMEM((be, t), BF16)],
        compiler_params=_cparams(("parallel", "arbitrary")),
    )(x2d, xnt, thr, s2, e1, e2, down_bf16, upt_bf16)


def _peer_ffn(x2d, gain, w_query, sub_keys, expert_down, expert_up):
    xnt, thr, s2, e1, e2 = _peer_route(x2d, gain, w_query.astype(BF16), sub_keys.astype(BF16))
    return _peer_main(x2d, xnt, thr, s2, e1, e2, expert_down.astype(BF16), expert_up.T.astype(BF16))


def _pad_cols(w, width):
    return jnp.pad(w, ((0, 0), (0, width - w.shape[1])))


def _swap_halves(w):
    half = w.shape[-1] // 2
    return jnp.concatenate([w[..., half:], w[..., :half]], axis=-1)


def _rope_gains(g):
    rope = g[MLA_NOPE:]
    return jnp.stack([g[:MLA_NOPE],
                      jnp.pad(rope, (0, LANES - MLA_ROPE)),
                      jnp.pad(_swap_halves(rope), (0, LANES - MLA_ROPE))])


def kernel(x, positions, lb_logits, l0_norm_mix, l0_w_in, l0_hgrn_out_norm, l0_gmlp_v_norm, l0_gmlp_w_s, l0_gmlp_b_s, l0_w_out, l0_norm_ffn, l0_peer_w_query, l0_peer_sub_keys, l0_peer_expert_down, l0_peer_expert_up, l1_norm_mix, l1_w_in, l1_mla_cq_norm, l1_mla_ckv_norm, l1_mla_w_uq, l1_mla_w_ukv, l1_mla_q_norm, l1_mla_k_norm, l1_moba_q_norm, l1_moba_k_norm, l1_w_out, l1_norm_ffn, l1_peer_w_query, l1_peer_sub_keys, l1_peer_expert_down, l1_peer_expert_up):
    batch, seq, d = x.shape
    n = batch * seq
    x2d = x.reshape(n, d)

    lb0 = jax.nn.softmax(lb_logits.astype(F32), axis=0)[0]
    z0 = _norm_matmul(x2d, l0_norm_mix, l0_w_in.astype(BF16))
    out_a, out_b = _l0_mixer(z0, lb0, l0_hgrn_out_norm, l0_gmlp_v_norm, l0_gmlp_w_s, l0_gmlp_b_s, batch, seq)
    half = out_a.shape[1]
    w_out0 = l0_w_out.astype(BF16)
    x2d = _out_proj(x2d, out_a, out_b, w_out0[:half], w_out0[half:])
    x2d = _peer_ffn(x2d, l0_norm_ffn, l0_peer_w_query, l0_peer_sub_keys, l0_peer_expert_down, l0_peer_expert_up)

    q_lora, kv_lora = l1_mla_cq_norm.shape[0], l1_mla_ckv_norm.shape[0]
    c0 = q_lora + kv_lora
    dw = MOBA_HEADS * LANES
    w1 = l1_w_in
    k_pe_w = w1[:, c0:c0 + MLA_ROPE]
    moba_w = w1[:, c0 + MLA_ROPE:]
    w1p = jnp.concatenate([moba_w, w1[:, :c0], _pad_cols(k_pe_w, LANES),
                           _pad_cols(_swap_halves(k_pe_w), LANES)], axis=1).astype(BF16)
    z1 = _norm_matmul(x2d, l1_norm_mix, w1p)

    half_r = MLA_ROPE // 2
    inv_freq = ROPE_THETA ** (-jnp.arange(half_r, dtype=F32) / half_r)
    ang = positions.reshape(n, 1).astype(F32) * inv_freq
    zeros = jnp.zeros((n, LANES - MLA_ROPE), F32)
    cos_t = jnp.concatenate([jnp.cos(ang), jnp.cos(ang), zeros], axis=1)
    sin_t = jnp.concatenate([-jnp.sin(ang), jnp.sin(ang), zeros], axis=1)

    wuq = l1_mla_w_uq.reshape(q_lora, MLA_HEADS, MLA_QK)
    wuq_p = jnp.concatenate(
        [wuq[..., :MLA_NOPE],
         jnp.pad(wuq[..., MLA_NOPE:], ((0, 0), (0, 0), (0, LANES - MLA_ROPE))),
         jnp.pad(_swap_halves(wuq[..., MLA_NOPE:]), ((0, 0), (0, 0), (0, LANES - MLA_ROPE)))],
        axis=-1).reshape(q_lora, MLA_HEADS * 3 * LANES).astype(BF16)
    q_c, k_c, v_c = _mla_prep(z1, cos_t, sin_t, l1_mla_cq_norm.reshape(1, -1), l1_mla_ckv_norm.reshape(1, -1),
                              wuq_p, l1_mla_w_ukv.astype(BF16), _rope_gains(l1_mla_q_norm),
                              _rope_gains(l1_mla_k_norm), col0=3 * MOBA_HEADS)
    out_c = _mla_attention(q_c, k_c, v_c, batch, seq)

    nb = seq // MOBA_BLOCK
    qn, kn, vb, kmean = _moba_norm(z1, l1_moba_q_norm, l1_moba_k_norm,
                                   col_q=0, col_k=MOBA_HEADS, col_v=2 * MOBA_HEADS)
    out_d = _moba_attention(qn, kn, vb, kmean.reshape(batch, nb, dw), batch, nb)

    w_out1 = l1_w_out.astype(BF16)
    x2d = _out_proj(x2d, out_c, out_d, w_out1[:out_c.shape[1]], w_out1[out_c.shape[1]:])
    x2d = _peer_ffn(x2d, l1_norm_ffn, l1_peer_w_query, l1_peer_sub_keys, l1_peer_expert_down, l1_peer_expert_up)
    return x2d.reshape(batch, seq, d)
```

```python
import functools

import numpy as np
import jax
import jax.numpy as jnp
from jax import lax
from jax.experimental import pallas as pl
from jax.experimental.pallas import tpu as pltpu

F32 = jnp.float32
BF16 = jnp.bfloat16

RMS_EPS = 1e-6
LANES = 128
VMEM_LIMIT = 56 * 1024 * 1024

HGRN_HEADS = 4
GMLP_GROUPS = 4
MIX_CHUNK = 128
MLA_HEADS = 4
MLA_NOPE = 128
MLA_ROPE = 64
MLA_QK = MLA_NOPE + MLA_ROPE
ROPE_THETA = 10000.0
MOBA_HEADS = 4
MOBA_BLOCK = 256
MOBA_TOPK = 3
PEER_HEADS = 8
PEER_KEYS = 128
PEER_TOPK = 16
NEG_BIG = -1e30


def _cparams(semantics):
    return pltpu.CompilerParams(dimension_semantics=semantics, vmem_limit_bytes=VMEM_LIMIT)


def _dot(a, b):
    return jnp.dot(a, b, preferred_element_type=F32)


def _dot_nt(a, b):
    return lax.dot_general(a, b, (((1,), (1,)), ((), ())), preferred_element_type=F32)


def _dot_tn(a, b):
    return lax.dot_general(a, b, (((0,), (0,)), ((), ())), preferred_element_type=F32)


def _rms(x, gain):
    return x * lax.rsqrt(jnp.mean(x * x, axis=-1, keepdims=True) + RMS_EPS) * gain


def _gelu(x):
    c = np.float32(np.sqrt(2.0 / np.pi))
    half = 0.5 * x
    return half + half * jnp.tanh(x * (c + np.float32(c * 0.044715) * (x * x)))


def _silu(x):
    return x * jax.nn.sigmoid(x)


def _norm_matmul_body(x_ref, g_ref, w_ref, o_ref):
    y = _rms(x_ref[...], g_ref[...])
    o_ref[...] = _dot(y.astype(BF16), w_ref[...])


def _norm_matmul(x2d, gain, w_bf16, tm=512):
    n, d = x2d.shape
    m = w_bf16.shape[1]
    return pl.pallas_call(
        _norm_matmul_body,
        grid=(n // tm,),
        in_specs=[pl.BlockSpec((tm, d), lambda i: (i, 0)),
                  pl.BlockSpec((1, d), lambda i: (0, 0)),
                  pl.BlockSpec((d, m), lambda i: (0, 0))],
        out_specs=pl.BlockSpec((tm, m), lambda i: (i, 0)),
        out_shape=jax.ShapeDtypeStruct((n, m), F32),
        compiler_params=_cparams(("parallel",)),
    )(x2d, gain.reshape(1, d), w_bf16)


def _out_proj_body(x_ref, a_ref, b_ref, wa_ref, wb_ref, o_ref):
    o_ref[...] = (x_ref[...] + _dot(a_ref[...].astype(BF16), wa_ref[...])
                  + _dot(b_ref[...].astype(BF16), wb_ref[...]))


def _out_proj(x2d, a, b, wa_bf16, wb_bf16, tm=512):
    n, d = x2d.shape
    ka, kb = a.shape[1], b.shape[1]
    return pl.pallas_call(
        _out_proj_body,
        grid=(n // tm,),
        in_specs=[pl.BlockSpec((tm, d), lambda i: (i, 0)),
                  pl.BlockSpec((tm, ka), lambda i: (i, 0)),
                  pl.BlockSpec((tm, kb), lambda i: (i, 0)),
                  pl.BlockSpec((ka, d), lambda i: (0, 0)),
                  pl.BlockSpec((kb, d), lambda i: (0, 0))],
        out_specs=pl.BlockSpec((tm, d), lambda i: (i, 0)),
        out_shape=jax.ShapeDtypeStruct((n, d), F32),
        compiler_params=_cparams(("parallel",)),
    )(x2d, a, b, wa_bf16, wb_bf16)


_HGRN_LEVELS = (64, 32, 16, 8, 4, 2, 1)


def _hgrn_exponent_matrix(c):
    a = np.zeros((len(_HGRN_LEVELS) + 1, c, c), np.float32)
    r = np.arange(c)[:, None]
    u = np.arange(c)[None, :]
    a[0] = (u <= r)
    for li, w in enumerate(_HGRN_LEVELS):
        p = (r // (2 * w)) * (2 * w) + w
        second = r >= p
        a[li + 1] = np.where(second, (u >= p) & (u <= r), (u > r) & (u <= p - 1))
    return a.reshape(-1, c)


def _l0_mix_body(q_ref, fg_ref, inp_ref, og_ref, u_ref, v_ref, lb_ref, on_ref, vn_ref,
                 ws_ref, bs_ref, amat_ref, oa_ref, ob_ref, state_ref):
    c = MIX_CHUNK

    @pl.when(pl.program_id(2) == 0)
    def _():
        state_ref[...] = jnp.zeros_like(state_ref)

    row = lax.broadcasted_iota(jnp.int32, (c, c), 0)
    col = lax.broadcasted_iota(jnp.int32, (c, c), 1)
    rowc = lax.broadcasted_iota(jnp.int32, (c, 1), 0)

    q = q_ref[...]
    lb = lb_ref[...]
    f = lb + (1.0 - lb) * jax.nn.sigmoid(fg_ref[...])
    k = 1.0 - f
    g = jnp.log(f)
    val = _silu(inp_ref[...])
    g1 = g.astype(BF16)
    r1 = g - g1.astype(F32)
    g2 = r1.astype(BF16)
    g3 = (r1 - g2.astype(F32)).astype(BF16)
    amat = amat_ref[...]
    dall = _dot(amat, g1) + _dot(amat, g2) + _dot(amat, g3)
    b = dall[0:c]

    scores = jnp.where(row == col, _dot_nt(q.astype(BF16), k.astype(BF16)), 0.0)
    for li, w in enumerate(_HGRN_LEVELS):
        e = jnp.exp(dall[(li + 1) * c:(li + 2) * c])
        sh = int(np.log2(w))
        second = (jnp.right_shift(rowc, sh) & 1) == 1
        qt = jnp.where(second, q * e, 0.0).astype(BF16)
        kt = jnp.where(second, 0.0, k * e).astype(BF16)
        same = jnp.right_shift(row, sh + 1) == jnp.right_shift(col, sh + 1)
        scores = scores + jnp.where(same, _dot_nt(qt, kt), 0.0)
    val16 = val.astype(BF16)
    o = _dot(scores.astype(BF16), val16)
    state_t = state_ref[...]
    o = o + _dot_nt((q * jnp.exp(b)).astype(BF16), state_t.astype(BF16))
    b_last = b[c - 1:c]
    k_dec = (k * jnp.exp(b_last - b)).astype(BF16)
    state_ref[...] = state_t * jnp.exp(b_last) + _dot_tn(val16, k_dec)
    oa_ref[...] = _rms(o, on_ref[...]) * _silu(og_ref[...])

    u = _gelu(u_ref[...])
    vn = _rms(_gelu(v_ref[...]), vn_ref[...])
    w_causal = jnp.where(row >= col, ws_ref[0], 0.0).astype(BF16)
    sv = _dot(w_causal, vn.astype(BF16)) + bs_ref[0]
    ob_ref[...] = u * sv


def _l0_mixer(z, lb, out_norm, v_norm, w_s, b_s, batch, seq):
    n = z.shape[0]
    c = MIX_CHUNK
    nc = seq // c
    h = HGRN_HEADS
    amat = jnp.asarray(_hgrn_exponent_matrix(c), BF16)
    na = amat.shape[0]

    def zcol(group):
        return pl.BlockSpec((c, LANES), lambda b, hh, i, g=group: (b * nc + i, g * h + hh))

    def vec():
        return pl.BlockSpec((1, LANES), lambda b, hh, i: (0, hh))

    out_spec = pl.BlockSpec((c, LANES), lambda b, hh, i: (b * nc + i, hh))
    return pl.pallas_call(
        _l0_mix_body,
        grid=(batch, h, nc),
        in_specs=[zcol(0), zcol(1), zcol(2), zcol(3), zcol(4), zcol(5),
                  vec(), vec(), vec(),
                  pl.BlockSpec((1, c, c), lambda b, hh, i: (hh, 0, 0)),
                  pl.BlockSpec((1, c, 1), lambda b, hh, i: (hh, 0, 0)),
                  pl.BlockSpec((na, c), lambda b, hh, i: (0, 0))],
        out_specs=[out_spec, out_spec],
        out_shape=[jax.ShapeDtypeStruct((n, h * LANES), F32),
                   jax.ShapeDtypeStruct((n, h * LANES), F32)],
        scratch_shapes=[pltpu.VMEM((LANES, LANES), F32)],
        compiler_params=_cparams(("parallel", "parallel", "arbitrary")),
    )(z, z, z, z, z, z, lb.reshape(1, -1), out_norm.reshape(1, -1), v_norm.reshape(1, -1),
      w_s, b_s.reshape(GMLP_GROUPS, c, 1), amat)


def _mla_prep_body(cq_ref, ckv_ref, kpe_ref, kpes_ref, cos_ref, sin_ref, cqn_ref, ckvn_ref,
                   wuq_ref, wukv_ref, gq_ref, gk_ref, q_out, k_out, v_out):
    cos = cos_ref[...]
    sin = sin_ref[...]
    qn = _rms(cq_ref[...], cqn_ref[...])
    kvn = _rms(ckv_ref[...], ckvn_ref[...])
    qall = _dot(qn.astype(BF16), wuq_ref[...])
    kvall = _dot(kvn.astype(BF16), wukv_ref[...])
    kpe = kpe_ref[...]
    kpes = kpes_ref[...]
    kpe_sq = jnp.sum(kpe * kpe, axis=-1, keepdims=True)
    gq = gq_ref[...]
    gk = gk_ref[...]
    scale = np.float32(MLA_QK ** -0.5)
    for h in range(MLA_HEADS):
        nope = qall[:, 3 * h * LANES:(3 * h + 1) * LANES]
        rp = qall[:, (3 * h + 1) * LANES:(3 * h + 2) * LANES]
        rps = qall[:, (3 * h + 2) * LANES:(3 * h + 3) * LANES]
        ms = (jnp.sum(nope * nope, axis=-1, keepdims=True)
              + jnp.sum(rp * rp, axis=-1, keepdims=True)) * np.float32(1.0 / MLA_QK)
        r = lax.rsqrt(ms + RMS_EPS) * scale
        q_out[:, 2 * h * LANES:(2 * h + 1) * LANES] = (nope * r * gq[0:1]).astype(BF16)
        q_out[:, (2 * h + 1) * LANES:(2 * h + 2) * LANES] = (
            (rp * r * gq[1:2]) * cos + (rps * r * gq[2:3]) * sin).astype(BF16)
        knope = kvall[:, 2 * h * LANES:(2 * h + 1) * LANES]
        ms = (jnp.sum(knope * knope, axis=-1, keepdims=True) + kpe_sq) * np.float32(1.0 / MLA_QK)
        r = lax.rsqrt(ms + RMS_EPS)
        k_out[:, 2 * h * LANES:(2 * h + 1) * LANES] = (knope * r * gk[0:1]).astype(BF16)
        k_out[:, (2 * h + 1) * LANES:(2 * h + 2) * LANES] = (
            (kpe * r * gk[1:2]) * cos + (kpes * r * gk[2:3]) * sin).astype(BF16)
        v_out[:, h * LANES:(h + 1) * LANES] = kvall[:, (2 * h + 1) * LANES:(2 * h + 2) * LANES].astype(BF16)


def _mla_prep(z, cos, sin, cq_norm, ckv_norm, wuq_p, wukv, gq, gk, col0, tm=256):
    n = z.shape[0]
    hq = MLA_HEADS * 2 * LANES
    full = lambda a: pl.BlockSpec(a.shape, lambda i: (0,) * a.ndim)
    return pl.pallas_call(
        _mla_prep_body,
        grid=(n // tm,),
        in_specs=[pl.BlockSpec((tm, 2 * LANES), lambda i: (i, col0 // 2)),
                  pl.BlockSpec((tm, LANES), lambda i: (i, col0 + 2)),
                  pl.BlockSpec((tm, LANES), lambda i: (i, col0 + 3)),
                  pl.BlockSpec((tm, LANES), lambda i: (i, col0 + 4)),
                  pl.BlockSpec((tm, LANES), lambda i: (i, 0)),
                  pl.BlockSpec((tm, LANES), lambda i: (i, 0)),
                  full(cq_norm), full(ckv_norm), full(wuq_p), full(wukv), full(gq), full(gk)],
        out_specs=[pl.BlockSpec((tm, hq), lambda i: (i, 0)),
                   pl.BlockSpec((tm, hq), lambda i: (i, 0)),
                   pl.BlockSpec((tm, MLA_HEADS * LANES), lambda i: (i, 0))],
        out_shape=[jax.ShapeDtypeStruct((n, hq), BF16),
                   jax.ShapeDtypeStruct((n, hq), BF16),
                   jax.ShapeDtypeStruct((n, MLA_HEADS * LANES), BF16)],
        compiler_params=_cparams(("parallel",)),
    )(z, z, z, z, cos, sin, cq_norm, ckv_norm, wuq_p, wukv, gq, gk)


def _flash_update(carry, q, tiles):
    m_prev, l_prev, acc = carry
    scores = [jnp.where(mask, _dot_nt(q, k), NEG_BIG) for k, _, mask in tiles]
    m_new = m_prev
    for s in scores:
        m_new = jnp.maximum(m_new, jnp.max(s, axis=-1, keepdims=True))
    alpha = jnp.exp(m_prev - m_new)
    l_new = alpha * l_prev
    acc = alpha * acc
    for s, (_, v, _) in zip(scores, tiles):
        p = jnp.exp(s - m_new)
        l_new = l_new + jnp.sum(p, axis=-1, keepdims=True)
        acc = acc + _dot(p.astype(BF16), v)
    return m_new, l_new, acc


def _flash_start(tq, dv):
    return (jnp.full((tq, 1), NEG_BIG, F32), jnp.zeros((tq, 1), F32), jnp.zeros((tq, dv), F32))


_MLA_GROUP = 1


def _mla_attn_body(q_ref, k_ref, v_ref, o_ref):
    qi = pl.program_id(2)
    q = q_ref[...]
    t = q.shape[0]
    row = lax.broadcasted_iota(jnp.int32, (t, t), 0)
    col = lax.broadcasted_iota(jnp.int32, (t, t), 1)
    ahead = col - row
    groups = (qi + _MLA_GROUP) // _MLA_GROUP

    def step(it, carry):
        tiles = []
        for c in range(_MLA_GROUP):
            n = (groups - 1 - it) * _MLA_GROUP + c
            start = pl.multiple_of(n * t, t)
            tiles.append((k_ref[pl.ds(start, t), :], v_ref[pl.ds(start, t), :], ahead <= (qi - n) * t))
        return _flash_update(carry, q, tiles)

    carry = lax.fori_loop(0, groups, step, _flash_start(t, v_ref.shape[1]))
    o_ref[...] = carry[2] / carry[1]


def _mla_attention(q, k, v, batch, seq, t=512):
    n = q.shape[0]
    nt = seq // t
    return pl.pallas_call(
        _mla_attn_body,
        grid=(batch, MLA_HEADS, nt),
        in_specs=[pl.BlockSpec((t, 2 * LANES), lambda b, h, qi: (b * nt + qi, h)),
                  pl.BlockSpec((seq, 2 * LANES), lambda b, h, qi: (b, h)),
                  pl.BlockSpec((seq, LANES), lambda b, h, qi: (b, h))],
        out_specs=pl.BlockSpec((t, LANES), lambda b, h, qi: (b * nt + qi, h)),
        out_shape=jax.ShapeDtypeStruct((n, MLA_HEADS * LANES), F32),
        compiler_params=_cparams(("parallel", "parallel", "arbitrary")),
    )(q, k, v)


def _moba_norm_body(q_ref, k_ref, v_ref, gq_ref, gk_ref, qn_ref, kn_ref, vb_ref, km_ref):
    qn_ref[...] = (_rms(q_ref[...], gq_ref[...]) * np.float32(LANES ** -0.5)).astype(BF16)
    kn = _rms(k_ref[...], gk_ref[...])
    kn_ref[...] = kn.astype(BF16)
    vb_ref[...] = v_ref[...].astype(BF16)
    km_ref[0] = jnp.mean(kn, axis=0, keepdims=True)


def _moba_norm(z, gq, gk, col_q, col_k, col_v):
    n = z.shape[0]
    h = MOBA_HEADS
    nblk = n // MOBA_BLOCK
    blk = lambda c0: pl.BlockSpec((MOBA_BLOCK, LANES), lambda i, hh: (i, c0 + hh))
    vec = pl.BlockSpec((1, LANES), lambda i, hh: (0, 0))
    act = jax.ShapeDtypeStruct((n, h * LANES), BF16)
    return pl.pallas_call(
        _moba_norm_body,
        grid=(nblk, h),
        in_specs=[blk(col_q), blk(col_k), blk(col_v), vec, vec],
        out_specs=[blk(0), blk(0), blk(0), pl.BlockSpec((1, 1, LANES), lambda i, hh: (i, 0, hh))],
        out_shape=[act, act, act, jax.ShapeDtypeStruct((nblk, 1, h * LANES), F32)],
        compiler_params=_cparams(("parallel", "parallel")),
    )(z, z, z, gq.reshape(1, -1), gk.reshape(1, -1))


_MOBA_GROUP = 4


def _moba_attn_body(q_ref, k_ref, v_ref, km_ref, o_ref, *, nb):
    j = pl.program_id(2)
    q = q_ref[...]
    t = q.shape[0]

    km = km_ref[0]
    if nb < LANES:
        km = jnp.concatenate([km, jnp.zeros((LANES - nb, km.shape[1]), F32)], axis=0)
    gate = _dot_nt(q, km.astype(BF16))
    lane = lax.broadcasted_iota(jnp.int32, gate.shape, 1)
    past = lane < j
    masked = jnp.where(past, gate, -jnp.inf)
    s = masked
    kth = None
    for _ in range(MOBA_TOPK):
        kth = jnp.max(s, axis=-1, keepdims=True)
        s = jnp.where(s == kth, -jnp.inf, s)
    sel = jnp.where((past & (masked >= kth)) | (lane == j), 1.0, 0.0)

    row = lax.broadcasted_iota(jnp.int32, (t, t), 0)
    col = lax.broadcasted_iota(jnp.int32, (t, t), 1)
    ahead = col - row
    groups = (j + _MOBA_GROUP) // _MOBA_GROUP

    def step(it, carry):
        tiles = []
        for c in range(_MOBA_GROUP):
            n = (groups - 1 - it) * _MOBA_GROUP + c
            start = pl.multiple_of(n * t, t)
            picked = jnp.sum(jnp.where(lane == n, sel, 0.0), axis=-1, keepdims=True) > 0.5
            mask = picked & (ahead <= jnp.where(n == j, 0, t))
            tiles.append((k_ref[pl.ds(start, t), :], v_ref[pl.ds(start, t), :], mask))
        return _flash_update(carry, q, tiles)

    carry = lax.fori_loop(0, groups, step, _flash_start(t, v_ref.shape[1]))
    o_ref[...] = carry[2] / carry[1]


def _moba_attention(qn, kn, vb, kmean, batch, nb):
    n = qn.shape[0]
    h = MOBA_HEADS
    blk = MOBA_BLOCK
    seq = nb * blk
    return pl.pallas_call(
        functools.partial(_moba_attn_body, nb=nb),
        grid=(batch, h, nb),
        in_specs=[pl.BlockSpec((blk, LANES), lambda b, hh, j: (b * nb + j, hh)),
                  pl.BlockSpec((seq, LANES), lambda b, hh, j: (b, hh)),
                  pl.BlockSpec((seq, LANES), lambda b, hh, j: (b, hh)),
                  pl.BlockSpec((1, nb, LANES), lambda b, hh, j: (b, 0, hh))],
        out_specs=pl.BlockSpec((blk, LANES), lambda b, hh, j: (b * nb + j, hh)),
        out_shape=jax.ShapeDtypeStruct((n, h * LANES), F32),
        compiler_params=_cparams(("parallel", "parallel", "arbitrary")),
    )(qn, kn, vb, kmean)


_SUBLANES = 8


def _all_sublanes(x, op):
    for shift in (4, 2, 1):
        x = op(x, pltpu.roll(x, shift, axis=0))
    return x


def _bitonic_merge(v):
    v = list(v)
    for j in (8, 4, 2, 1):
        for i in range(16):
            if i & j == 0:
                v[i], v[i | j] = jnp.maximum(v[i], v[i | j]), jnp.minimum(v[i], v[i | j])
    return v


def _top16(v):
    v = list(v)
    k = 2
    while k <= 16:
        j = k // 2
        while j >= 1:
            for i in range(16):
                l = i ^ j
                if l > i:
                    hi, lo = jnp.maximum(v[i], v[l]), jnp.minimum(v[i], v[l])
                    v[i], v[l] = (hi, lo) if i & k == 0 else (lo, hi)
            j //= 2
        k *= 2
    for shift in (4, 2, 1):
        v = _bitonic_merge([jnp.maximum(v[i], pltpu.roll(v[15 - i], shift, axis=0)) for i in range(16)])
    return v


def _next_largest(tiles, kth):
    count = jnp.zeros_like(kth)
    below = jnp.full_like(kth, -jnp.inf)
    for x in tiles:
        count = count + jnp.where(x >= kth, 1.0, 0.0)
        below = jnp.maximum(below, jnp.where(x < kth, x, -jnp.inf))
    count = _all_sublanes(count, jnp.add)
    below = _all_sublanes(below, jnp.maximum)
    return jnp.where(count > PEER_TOPK + 0.5, kth, below)


def _on_sublanes(tiles):
    sub = lax.broadcasted_iota(jnp.int32, tiles[0].shape, 0)
    out = tiles[0]
    for b in range(1, _SUBLANES):
        out = jnp.where(sub == b, tiles[b], out)
    return out


def _peer_route_body(x_ref, g_ref, wq_ref, sk_ref, xnt_ref, thr_ref, s2_ref, e1_ref, e2_ref, q_scr):
    h = pl.program_id(1)

    @pl.when(h == 0)
    def _():
        xn = _rms(x_ref[...], g_ref[...])
        xnt_ref[...] = xn.T.astype(BF16)
        q = _dot(xn.astype(BF16), wq_ref[...])
        for hp in range(2 * PEER_HEADS):
            q_scr[hp] = q[:, hp * LANES:(hp + 1) * LANES].astype(BF16)

    s1_all = _dot_nt(sk_ref[0, 0], q_scr[2 * h])
    s2_all = _dot_nt(sk_ref[0, 1], q_scr[2 * h + 1])
    for lt in range(s1_all.shape[1] // LANES):
        lanes = slice(lt * LANES, (lt + 1) * LANES)
        thr, e1, e2 = _peer_route_tile(s1_all[:, lanes], s2_all[:, lanes])
        thr_ref[0, lt] = thr
        s2_ref[0, lt] = s2_all[:, lanes]
        e1_ref[0, lt] = e1
        e2_ref[0, lt] = e2


def _peer_route_tile(s1, s2):
    tiles1 = [s1[v * _SUBLANES:(v + 1) * _SUBLANES] for v in range(PEER_KEYS // _SUBLANES)]
    tiles2 = [s2[v * _SUBLANES:(v + 1) * _SUBLANES] for v in range(PEER_KEYS // _SUBLANES)]
    top1 = _top16(tiles1)
    top2 = _top16(tiles2)
    next1 = _next_largest(tiles1, top1[-1])
    next2 = _next_largest(tiles2, top2[-1])
    col1 = [_on_sublanes(top1[:_SUBLANES]), _on_sublanes(top1[_SUBLANES:])]
    col2 = [_on_sublanes(top2[:_SUBLANES]), _on_sublanes(top2[_SUBLANES:])]
    cand = [top1[a] + col2[0] for a in range(_SUBLANES)] + [top1[0] + col2[1], top2[0] + col1[1]]
    pad = jnp.full_like(s1[:_SUBLANES], -jnp.inf)
    kth = _top16(cand + [pad] * (16 - len(cand)))[-1]
    nxt = jnp.maximum(_next_largest(cand, kth), jnp.maximum(next1 + top2[0], top1[0] + next2))
    tau = 0.5 * (kth + nxt)
    m1 = top1[0]
    m2 = top2[0]
    z = jnp.zeros_like(tau)
    for c in cand:
        z = z + jnp.where(c >= tau, jnp.exp(c - (m1 + m2)), 0.0)
    inv_z = 1.0 / _all_sublanes(z, jnp.add)
    thr = jnp.concatenate([tau - x for x in tiles1], axis=0)
    e1 = jnp.concatenate([jnp.exp(x - m1) * inv_z for x in tiles1], axis=0)
    e2 = jnp.concatenate([jnp.exp(x - m2) for x in tiles2], axis=0)
    return thr, e1, e2


def _peer_route(x2d, gain, wq_bf16, sk_bf16, t=512):
    n, d = x2d.shape
    hq = wq_bf16.shape[1]
    routed = jax.ShapeDtypeStruct((PEER_HEADS, n // LANES, PEER_KEYS, LANES), F32)
    rspec = pl.BlockSpec((1, t // LANES, PEER_KEYS, LANES), lambda i, h: (h, i, 0, 0))
    return pl.pallas_call(
        _peer_route_body,
        grid=(n // t, PEER_HEADS),
        in_specs=[pl.BlockSpec((t, d), lambda i, h: (i, 0)),
                  pl.BlockSpec((1, d), lambda i, h: (0, 0)),
                  pl.BlockSpec((d, hq), lambda i, h: (0, 0)),
                  pl.BlockSpec((1, 2, PEER_KEYS, LANES), lambda i, h: (h, 0, 0, 0))],
        out_specs=[pl.BlockSpec((d, t), lambda i, h: (0, i)), rspec, rspec, rspec, rspec],
        out_shape=[jax.ShapeDtypeStruct((d, n), BF16), routed, routed, routed, routed],
        scratch_shapes=[pltpu.VMEM((hq // LANES, t, LANES), BF16)],
        compiler_params=_cparams(("parallel", "arbitrary")),
    )(x2d, gain.reshape(1, d), wq_bf16, sk_bf16)


_PEER_ROWS = 2


def _peer_main_body(x_ref, xnt_ref, thr_ref, s2_ref, e1_ref, e2_ref, down_ref, upt_ref, o_ref,
                    acc_ref, act_ref, p_ref, *, bi):
    e = pl.program_id(1)
    t = act_ref.shape[1]
    nj = PEER_KEYS // _SUBLANES
    row0 = pl.multiple_of(e * bi, bi)

    @pl.when(e == 0)
    def _():
        acc_ref[...] = jnp.zeros_like(acc_ref)

    act_ref[...] = _gelu(_dot(down_ref[...], xnt_ref[...]))

    for lt in range(t // LANES):
        lanes = slice(lt * LANES, (lt + 1) * LANES)

        def row_group(g, carry, lt=lt, lanes=lanes):
            i0 = g * _PEER_ROWS
            w = [[jnp.zeros((_SUBLANES, LANES), F32) for _ in range(nj)] for _ in range(_PEER_ROWS)]
            for h in range(PEER_HEADS):
                thr = [jnp.broadcast_to(thr_ref[h, lt, pl.ds(row0 + i0 + r, 1), :], (_SUBLANES, LANES))
                       for r in range(_PEER_ROWS)]
                e1 = [jnp.broadcast_to(e1_ref[h, lt, pl.ds(row0 + i0 + r, 1), :], (_SUBLANES, LANES))
                      for r in range(_PEER_ROWS)]
                for jv in range(nj):
                    rows = slice(jv * _SUBLANES, (jv + 1) * _SUBLANES)
                    s2 = s2_ref[h, lt, rows, :]
                    e2 = e2_ref[h, lt, rows, :]
                    for r in range(_PEER_ROWS):
                        w[r][jv] = w[r][jv] + jnp.where(s2 >= thr[r], e2, 0.0) * e1[r]
            for r in range(_PEER_ROWS):
                for jp in range(nj // 2):
                    start = pl.multiple_of((i0 + r) * PEER_KEYS + jp * 2 * _SUBLANES, 2 * _SUBLANES)
                    rows = pl.ds(start, 2 * _SUBLANES)
                    wt = jnp.concatenate([w[r][2 * jp], w[r][2 * jp + 1]], axis=0)
                    p_ref[rows, lanes] = (wt * act_ref[rows, lanes]).astype(BF16)
            return carry

        lax.fori_loop(0, bi // _PEER_ROWS, row_group, 0)

    acc_ref[...] += _dot(upt_ref[...], p_ref[...])

    @pl.when(e == pl.num_programs(1) - 1)
    def _():
        o_ref[...] = x_ref[...] + acc_ref[...].T


def _peer_main(x2d, xnt, thr, s2, e1, e2, down_bf16, upt_bf16, t=512, bi=8):
    n, d = x2d.shape
    ne = down_bf16.shape[0]
    be = bi * PEER_KEYS
    rspec = pl.BlockSpec((PEER_HEADS, t // LANES, PEER_KEYS, LANES), lambda i, e: (0, i, 0, 0))
    return pl.pallas_call(
        functools.partial(_peer_main_body, bi=bi),
        grid=(n // t, ne // be),
        in_specs=[pl.BlockSpec((t, d), lambda i, e: (i, 0)),
                  pl.BlockSpec((d, t), lambda i, e: (0, i)),
                  rspec, rspec, rspec, rspec,
                  pl.BlockSpec((be, d), lambda i, e: (e, 0)),
                  pl.BlockSpec((d, be), lambda i, e: (0, e))],
        out_specs=pl.BlockSpec((t, d), lambda i, e: (i, 0)),
        out_shape=jax.ShapeDtypeStruct((n, d), F32),
        scratch_shapes=[pltpu.VMEM((d, t), F32), pltpu.VMEM((be, t), F32), pltpu.VMEM((be, t), BF16)],
        compiler_params=_cparams(("parallel", "arbitrary")),
    )(x2d, xnt, thr, s2, e1, e2, down_bf16, upt_bf16)


def _peer_ffn(x2d, gain, w_query, sub_keys, expert_down, expert_up):
    xnt, thr, s2, e1, e2 = _peer_route(x2d, gain, w_query.astype(BF16), sub_keys.astype(BF16))
    return _peer_main(x2d, xnt, thr, s2, e1, e2, expert_down.astype(BF16), expert_up.T.astype(BF16))


def _pad_cols(w, width):
    return jnp.pad(w, ((0, 0), (0, width - w.shape[1])))


def _swap_halves(w):
    half = w.shape[-1] // 2
    return jnp.concatenate([w[..., half:], w[..., :half]], axis=-1)


def _rope_gains(g):
    rope = g[MLA_NOPE:]
    return jnp.stack([g[:MLA_NOPE],
                      jnp.pad(rope, (0, LANES - MLA_ROPE)),
                      jnp.pad(_swap_halves(rope), (0, LANES - MLA_ROPE))])


def kernel(x, positions, lb_logits, l0_norm_mix, l0_w_in, l0_hgrn_out_norm, l0_gmlp_v_norm, l0_gmlp_w_s, l0_gmlp_b_s, l0_w_out, l0_norm_ffn, l0_peer_w_query, l0_peer_sub_keys, l0_peer_expert_down, l0_peer_expert_up, l1_norm_mix, l1_w_in, l1_mla_cq_norm, l1_mla_ckv_norm, l1_mla_w_uq, l1_mla_w_ukv, l1_mla_q_norm, l1_mla_k_norm, l1_moba_q_norm, l1_moba_k_norm, l1_w_out, l1_norm_ffn, l1_peer_w_query, l1_peer_sub_keys, l1_peer_expert_down, l1_peer_expert_up):
    batch, seq, d = x.shape
    n = batch * seq
    x2d = x.reshape(n, d)

    lb0 = jax.nn.softmax(lb_logits.astype(F32), axis=0)[0]
    z0 = _norm_matmul(x2d, l0_norm_mix, l0_w_in.astype(BF16))
    out_a, out_b = _l0_mixer(z0, lb0, l0_hgrn_out_norm, l0_gmlp_v_norm, l0_gmlp_w_s, l0_gmlp_b_s, batch, seq)
    half = out_a.shape[1]
    w_out0 = l0_w_out.astype(BF16)
    x2d = _out_proj(x2d, out_a, out_b, w_out0[:half], w_out0[half:])
    x2d = _peer_ffn(x2d, l0_norm_ffn, l0_peer_w_query, l0_peer_sub_keys, l0_peer_expert_down, l0_peer_expert_up)

    q_lora, kv_lora = l1_mla_cq_norm.shape[0], l1_mla_ckv_norm.shape[0]
    c0 = q_lora + kv_lora
    dw = MOBA_HEADS * LANES
    w1 = l1_w_in
    k_pe_w = w1[:, c0:c0 + MLA_ROPE]
    moba_w = w1[:, c0 + MLA_ROPE:]
    w1p = jnp.concatenate([moba_w, w1[:, :c0], _pad_cols(k_pe_w, LANES),
                           _pad_cols(_swap_halves(k_pe_w), LANES)], axis=1).astype(BF16)
    z1 = _norm_matmul(x2d, l1_norm_mix, w1p)

    half_r = MLA_ROPE // 2
    inv_freq = ROPE_THETA ** (-jnp.arange(half_r, dtype=F32) / half_r)
    ang = positions.reshape(n, 1).astype(F32) * inv_freq
    zeros = jnp.zeros((n, LANES - MLA_ROPE), F32)
    cos_t = jnp.concatenate([jnp.cos(ang), jnp.cos(ang), zeros], axis=1)
    sin_t = jnp.concatenate([-jnp.sin(ang), jnp.sin(ang), zeros], axis=1)

    wuq = l1_mla_w_uq.reshape(q_lora, MLA_HEADS, MLA_QK)
    wuq_p = jnp.concatenate(
        [wuq[..., :MLA_NOPE],
         jnp.pad(wuq[..., MLA_NOPE:], ((0, 0), (0, 0), (0, LANES - MLA_ROPE))),
         jnp.pad(_swap_halves(wuq[..., MLA_NOPE:]), ((0, 0), (0, 0), (0, LANES - MLA_ROPE)))],
        axis=-1).reshape(q_lora, MLA_HEADS * 3 * LANES).astype(BF16)
    q_c, k_c, v_c = _mla_prep(z1, cos_t, sin_t, l1_mla_cq_norm.reshape(1, -1), l1_mla_ckv_norm.reshape(1, -1),
                              wuq_p, l1_mla_w_ukv.astype(BF16), _rope_gains(l1_mla_q_norm),
                              _rope_gains(l1_mla_k_norm), col0=3 * MOBA_HEADS)
    out_c = _mla_attention(q_c, k_c, v_c, batch, seq)

    nb = seq // MOBA_BLOCK
    qn, kn, vb, kmean = _moba_norm(z1, l1_moba_q_norm, l1_moba_k_norm,
                                   col_q=0, col_k=MOBA_HEADS, col_v=2 * MOBA_HEADS)
    out_d = _moba_attention(qn, kn, vb, kmean.reshape(batch, nb, dw), batch, nb)

    w_out1 = l1_w_out.astype(BF16)
    x2d = _out_proj(x2d, out_c, out_d, w_out1[:out_c.shape[1]], w_out1[out_c.shape[1]:])
    x2d = _peer_ffn(x2d, l1_norm_ffn, l1_peer_w_query, l1_peer_sub_keys, l1_peer_expert_down, l1_peer_expert_up)
    return x2d.reshape(batch, seq, d)
```

```python
import functools

import numpy as np
import jax
import jax.numpy as jnp
from jax import lax
from jax.experimental import pallas as pl
from jax.experimental.pallas import tpu as pltpu

F32 = jnp.float32
BF16 = jnp.bfloat16

RMS_EPS = 1e-6
LANES = 128
VMEM_LIMIT = 56 * 1024 * 1024

HGRN_HEADS = 4
GMLP_GROUPS = 4
MIX_CHUNK = 128
MLA_HEADS = 4
MLA_NOPE = 128
MLA_ROPE = 64
MLA_QK = MLA_NOPE + MLA_ROPE
ROPE_THETA = 10000.0
MOBA_HEADS = 4
MOBA_BLOCK = 256
MOBA_TOPK = 3
PEER_HEADS = 8
PEER_KEYS = 128
PEER_TOPK = 16
NEG_BIG = -1e30


def _cparams(semantics):
    return pltpu.CompilerParams(dimension_semantics=semantics, vmem_limit_bytes=VMEM_LIMIT)


def _dot(a, b):
    return jnp.dot(a, b, preferred_element_type=F32)


def _dot_nt(a, b):
    return lax.dot_general(a, b, (((1,), (1,)), ((), ())), preferred_element_type=F32)


def _dot_tn(a, b):
    return lax.dot_general(a, b, (((0,), (0,)), ((), ())), preferred_element_type=F32)


def _rms(x, gain):
    return x * lax.rsqrt(jnp.mean(x * x, axis=-1, keepdims=True) + RMS_EPS) * gain


def _gelu(x):
    c = np.float32(np.sqrt(2.0 / np.pi))
    half = 0.5 * x
    return half + half * jnp.tanh(x * (c + np.float32(c * 0.044715) * (x * x)))


def _silu(x):
    return x * jax.nn.sigmoid(x)


def _norm_matmul_body(x_ref, g_ref, w_ref, o_ref):
    y = _rms(x_ref[...], g_ref[...])
    o_ref[...] = _dot(y.astype(BF16), w_ref[...])


def _norm_matmul(x2d, gain, w_bf16, tm=512):
    n, d = x2d.shape
    m = w_bf16.shape[1]
    return pl.pallas_call(
        _norm_matmul_body,
        grid=(n // tm,),
        in_specs=[pl.BlockSpec((tm, d), lambda i: (i, 0)),
                  pl.BlockSpec((1, d), lambda i: (0, 0)),
                  pl.BlockSpec((d, m), lambda i: (0, 0))],
        out_specs=pl.BlockSpec((tm, m), lambda i: (i, 0)),
        out_shape=jax.ShapeDtypeStruct((n, m), F32),
        compiler_params=_cparams(("parallel",)),
    )(x2d, gain.reshape(1, d), w_bf16)


def _out_proj_body(x_ref, a_ref, b_ref, wa_ref, wb_ref, o_ref):
    o_ref[...] = (x_ref[...] + _dot(a_ref[...].astype(BF16), wa_ref[...])
                  + _dot(b_ref[...].astype(BF16), wb_ref[...]))


def _out_proj(x2d, a, b, wa_bf16, wb_bf16, tm=512):
    n, d = x2d.shape
    ka, kb = a.shape[1], b.shape[1]
    return pl.pallas_call(
        _out_proj_body,
        grid=(n // tm,),
        in_specs=[pl.BlockSpec((tm, d), lambda i: (i, 0)),
                  pl.BlockSpec((tm, ka), lambda i: (i, 0)),
                  pl.BlockSpec((tm, kb), lambda i: (i, 0)),
                  pl.BlockSpec((ka, d), lambda i: (0, 0)),
                  pl.BlockSpec((kb, d), lambda i: (0, 0))],
        out_specs=pl.BlockSpec((tm, d), lambda i: (i, 0)),
        out_shape=jax.ShapeDtypeStruct((n, d), F32),
        compiler_params=_cparams(("parallel",)),
    )(x2d, a, b, wa_bf16, wb_bf16)


_HGRN_LEVELS = (64, 32, 16, 8, 4, 2, 1)


def _hgrn_exponent_matrix(c):
    a = np.zeros((len(_HGRN_LEVELS) + 1, c, c), np.float32)
    r = np.arange(c)[:, None]
    u = np.arange(c)[None, :]
    a[0] = (u <= r)
    for li, w in enumerate(_HGRN_LEVELS):
        p = (r // (2 * w)) * (2 * w) + w
        second = r >= p
        a[li + 1] = np.where(second, (u >= p) & (u <= r), (u > r) & (u <= p - 1))
    return a.reshape(-1, c)


_L0_HEADS_PER_STEP = 2


def _hgrn_head(q, fg, inp, og, lb, out_gain, amat, state_t):
    c = MIX_CHUNK
    row = lax.broadcasted_iota(jnp.int32, (c, c), 0)
    col = lax.broadcasted_iota(jnp.int32, (c, c), 1)
    rowc = lax.broadcasted_iota(jnp.int32, (c, 1), 0)
    f = lb + (1.0 - lb) * jax.nn.sigmoid(fg)
    k = 1.0 - f
    g = jnp.log(f)
    val = _silu(inp)
    g1 = g.astype(BF16)
    r1 = g - g1.astype(F32)
    g2 = r1.astype(BF16)
    g3 = (r1 - g2.astype(F32)).astype(BF16)
    dall = _dot(amat, g1) + _dot(amat, g2) + _dot(amat, g3)
    b = dall[0:c]

    scores = jnp.where(row == col, _dot_nt(q.astype(BF16), k.astype(BF16)), 0.0)
    for li, w in enumerate(_HGRN_LEVELS):
        e = jnp.exp(dall[(li + 1) * c:(li + 2) * c])
        sh = int(np.log2(w))
        second = (jnp.right_shift(rowc, sh) & 1) == 1
        qt = jnp.where(second, q * e, 0.0).astype(BF16)
        kt = jnp.where(second, 0.0, k * e).astype(BF16)
        same = jnp.right_shift(row, sh + 1) == jnp.right_shift(col, sh + 1)
        scores = scores + jnp.where(same, _dot_nt(qt, kt), 0.0)
    val16 = val.astype(BF16)
    o = _dot(scores.astype(BF16), val16)
    o = o + _dot_nt((q * jnp.exp(b)).astype(BF16), state_t.astype(BF16))
    b_last = b[c - 1:c]
    k_dec = (k * jnp.exp(b_last - b)).astype(BF16)
    new_state = state_t * jnp.exp(b_last) + _dot_tn(val16, k_dec)
    return _rms(o, out_gain) * _silu(og), new_state


def _gmlp_group(u, v, v_gain, w_s, b_s):
    c = MIX_CHUNK
    row = lax.broadcasted_iota(jnp.int32, (c, c), 0)
    col = lax.broadcasted_iota(jnp.int32, (c, c), 1)
    vn = _rms(_gelu(v), v_gain)
    w_causal = jnp.where(row >= col, w_s, 0.0).astype(BF16)
    return _gelu(u) * (_dot(w_causal, vn.astype(BF16)) + b_s)


def _l0_mix_body(q_ref, fg_ref, inp_ref, og_ref, u_ref, v_ref, lb_ref, on_ref, vn_ref,
                 ws_ref, bs_ref, amat_ref, oa_ref, ob_ref, state_ref):
    @pl.when(pl.program_id(2) == 0)
    def _():
        state_ref[...] = jnp.zeros_like(state_ref)

    amat = amat_ref[...]
    for p in range(_L0_HEADS_PER_STEP):
        lanes = slice(p * LANES, (p + 1) * LANES)
        oa, new_state = _hgrn_head(q_ref[:, lanes], fg_ref[:, lanes], inp_ref[:, lanes], og_ref[:, lanes],
                                   lb_ref[:, lanes], on_ref[:, lanes], amat, state_ref[p])
        state_ref[p] = new_state
        oa_ref[:, lanes] = oa
        ob_ref[:, lanes] = _gmlp_group(u_ref[:, lanes], v_ref[:, lanes], vn_ref[:, lanes], ws_ref[p], bs_ref[p])


def _l0_mixer(z, lb, out_norm, v_norm, w_s, b_s, batch, seq):
    n = z.shape[0]
    c = MIX_CHUNK
    nc = seq // c
    hp = _L0_HEADS_PER_STEP
    width = hp * LANES
    steps = HGRN_HEADS // hp
    amat = jnp.asarray(_hgrn_exponent_matrix(c), BF16)
    na = amat.shape[0]

    def zcol(group):
        return pl.BlockSpec((c, width), lambda b, hh, i, g=group: (b * nc + i, g * steps + hh))

    def vec():
        return pl.BlockSpec((1, width), lambda b, hh, i: (0, hh))

    out_spec = pl.BlockSpec((c, width), lambda b, hh, i: (b * nc + i, hh))
    return pl.pallas_call(
        _l0_mix_body,
        grid=(batch, steps, nc),
        in_specs=[zcol(0), zcol(1), zcol(2), zcol(3), zcol(4), zcol(5),
                  vec(), vec(), vec(),
                  pl.BlockSpec((hp, c, c), lambda b, hh, i: (hh, 0, 0)),
                  pl.BlockSpec((hp, c, 1), lambda b, hh, i: (hh, 0, 0)),
                  pl.BlockSpec((na, c), lambda b, hh, i: (0, 0))],
        out_specs=[out_spec, out_spec],
        out_shape=[jax.ShapeDtypeStruct((n, HGRN_HEADS * LANES), F32),
                   jax.ShapeDtypeStruct((n, HGRN_HEADS * LANES), F32)],
        scratch_shapes=[pltpu.VMEM((hp, LANES, LANES), F32)],
        compiler_params=_cparams(("parallel", "parallel", "arbitrary")),
    )(z, z, z, z, z, z, lb.reshape(1, -1), out_norm.reshape(1, -1), v_norm.reshape(1, -1),
      w_s, b_s.reshape(GMLP_GROUPS, c, 1), amat)


def _mla_prep_body(cq_ref, ckv_ref, kpe_ref, kpes_ref, cos_ref, sin_ref, cqn_ref, ckvn_ref,
                   wuq_ref, wukv_ref, gq_ref, gk_ref, q_out, k_out, v_out):
    cos = cos_ref[...]
    sin = sin_ref[...]
    qn = _rms(cq_ref[...], cqn_ref[...])
    kvn = _rms(ckv_ref[...], ckvn_ref[...])
    qall = _dot(qn.astype(BF16), wuq_ref[...])
    kvall = _dot(kvn.astype(BF16), wukv_ref[...])
    kpe = kpe_ref[...]
    kpes = kpes_ref[...]
    kpe_sq = jnp.sum(kpe * kpe, axis=-1, keepdims=True)
    gq = gq_ref[...]
    gk = gk_ref[...]
    scale = np.float32(MLA_QK ** -0.5)
    for h in range(MLA_HEADS):
        nope = qall[:, 3 * h * LANES:(3 * h + 1) * LANES]
        rp = qall[:, (3 * h + 1) * LANES:(3 * h + 2) * LANES]
        rps = qall[:, (3 * h + 2) * LANES:(3 * h + 3) * LANES]
        ms = (jnp.sum(nope * nope, axis=-1, keepdims=True)
              + jnp.sum(rp * rp, axis=-1, keepdims=True)) * np.float32(1.0 / MLA_QK)
        r = lax.rsqrt(ms + RMS_EPS) * scale
        q_out[:, 2 * h * LANES:(2 * h + 1) * LANES] = (nope * r * gq[0:1]).astype(BF16)
        q_out[:, (2 * h + 1) * LANES:(2 * h + 2) * LANES] = (
            (rp * r * gq[1:2]) * cos + (rps * r * gq[2:3]) * sin).astype(BF16)
        knope = kvall[:, 2 * h * LANES:(2 * h + 1) * LANES]
        ms = (jnp.sum(knope * knope, axis=-1, keepdims=True) + kpe_sq) * np.float32(1.0 / MLA_QK)
        r = lax.rsqrt(ms + RMS_EPS)
        k_out[:, 2 * h * LANES:(2 * h + 1) * LANES] = (knope * r * gk[0:1]).astype(BF16)
        k_out[:, (2 * h + 1) * LANES:(2 * h + 2) * LANES] = (
            (kpe * r * gk[1:2]) * cos + (kpes * r * gk[2:3]) * sin).astype(BF16)
        v_out[:, h * LANES:(h + 1) * LANES] = kvall[:, (2 * h + 1) * LANES:(2 * h + 2) * LANES].astype(BF16)


def _mla_prep(z, cos, sin, cq_norm, ckv_norm, wuq_p, wukv, gq, gk, col0, tm=256):
    n = z.shape[0]
    hq = MLA_HEADS * 2 * LANES
    full = lambda a: pl.BlockSpec(a.shape, lambda i: (0,) * a.ndim)
    return pl.pallas_call(
        _mla_prep_body,
        grid=(n // tm,),
        in_specs=[pl.BlockSpec((tm, 2 * LANES), lambda i: (i, col0 // 2)),
                  pl.BlockSpec((tm, LANES), lambda i: (i, col0 + 2)),
                  pl.BlockSpec((tm, LANES), lambda i: (i, col0 + 3)),
                  pl.BlockSpec((tm, LANES), lambda i: (i, col0 + 4)),
                  pl.BlockSpec((tm, LANES), lambda i: (i, 0)),
                  pl.BlockSpec((tm, LANES), lambda i: (i, 0)),
                  full(cq_norm), full(ckv_norm), full(wuq_p), full(wukv), full(gq), full(gk)],
        out_specs=[pl.BlockSpec((tm, hq), lambda i: (i, 0)),
                   pl.BlockSpec((tm, hq), lambda i: (i, 0)),
                   pl.BlockSpec((tm, MLA_HEADS * LANES), lambda i: (i, 0))],
        out_shape=[jax.ShapeDtypeStruct((n, hq), BF16),
                   jax.ShapeDtypeStruct((n, hq), BF16),
                   jax.ShapeDtypeStruct((n, MLA_HEADS * LANES), BF16)],
        compiler_params=_cparams(("parallel",)),
    )(z, z, z, z, cos, sin, cq_norm, ckv_norm, wuq_p, wukv, gq, gk)


def _flash_scores(q, keys_masks):
    return [jnp.where(mask, _dot_nt(q, k), NEG_BIG) for k, mask in keys_masks]


def _flash_apply(state, scores, values):
    m_prev, l_prev, acc = state
    m_new = m_prev
    for s in scores:
        m_new = jnp.maximum(m_new, jnp.max(s, axis=-1, keepdims=True))
    alpha = jnp.exp(m_prev - m_new)
    l_new = alpha * l_prev
    acc = alpha * acc
    for s, v in zip(scores, values):
        p = jnp.exp(s - m_new)
        l_new = l_new + jnp.sum(p, axis=-1, keepdims=True)
        acc = acc + _dot(p.astype(BF16), v)
    return m_new, l_new, acc


def _flash_loop(steps, tq, dv, scores_of, values_of):
    state = (jnp.full((tq, 1), NEG_BIG, F32), jnp.zeros((tq, 1), F32), jnp.zeros((tq, dv), F32))
    state = lax.fori_loop(0, steps, lambda i, st: _flash_apply(st, scores_of(i), values_of(i)), state)
    return state[2] / state[1]


def _mla_attn_body(q_ref, k_ref, v_ref, o_ref, *, tk):
    qi = pl.program_id(2)
    q = q_ref[...]
    tq = q.shape[0]
    row = lax.broadcasted_iota(jnp.int32, (tq, tk), 0)
    col = lax.broadcasted_iota(jnp.int32, (tq, tk), 1)
    ahead = col - row

    def scores_of(n):
        start = pl.multiple_of(n * tk, tk)
        mask = ahead <= qi * tq - n * tk
        return _flash_scores(q, [(k_ref[pl.ds(start, tk), :], mask)])

    def values_of(n):
        return [v_ref[pl.ds(pl.multiple_of(n * tk, tk), tk), :]]

    o_ref[...] = _flash_loop((qi + 1) * (tq // tk), tq, v_ref.shape[1], scores_of, values_of)


def _mla_attention(q, k, v, batch, seq, tq=512, tk=512):
    n = q.shape[0]
    tq = min(tq, seq)
    nt = seq // tq
    return pl.pallas_call(
        functools.partial(_mla_attn_body, tk=tk),
        grid=(batch, MLA_HEADS, nt),
        in_specs=[pl.BlockSpec((tq, 2 * LANES), lambda b, h, qi: (b * nt + qi, h)),
                  pl.BlockSpec((seq, 2 * LANES), lambda b, h, qi: (b, h)),
                  pl.BlockSpec((seq, LANES), lambda b, h, qi: (b, h))],
        out_specs=pl.BlockSpec((tq, LANES), lambda b, h, qi: (b * nt + qi, h)),
        out_shape=jax.ShapeDtypeStruct((n, MLA_HEADS * LANES), F32),
        compiler_params=_cparams(("parallel", "parallel", "arbitrary")),
    )(q, k, v)


def _moba_norm_body(q_ref, k_ref, v_ref, gq_ref, gk_ref, qn_ref, kn_ref, vb_ref, km_ref, *, nb):
    block_id = pl.program_id(0) % nb
    lane = lax.broadcasted_iota(jnp.int32, (q_ref.shape[0], LANES), 1)
    one_hot = jnp.where(lane == block_id, 1.0, 0.0).astype(BF16)
    vb_ref[...] = v_ref[...].astype(BF16)
    for h in range(MOBA_HEADS):
        lanes = slice(h * LANES, (h + 1) * LANES)
        qn_ref[:, lanes] = (_rms(q_ref[:, lanes], gq_ref[...]) * np.float32(LANES ** -0.5)).astype(BF16)
        kn = _rms(k_ref[:, lanes], gk_ref[...])
        kn_ref[:, 2 * h * LANES:(2 * h + 1) * LANES] = kn.astype(BF16)
        kn_ref[:, (2 * h + 1) * LANES:(2 * h + 2) * LANES] = one_hot
        km_ref[0, :, lanes] = jnp.mean(kn, axis=0, keepdims=True)


def _moba_norm(z, gq, gk, col_q, col_k, col_v, nb):
    n = z.shape[0]
    h = MOBA_HEADS
    width = h * LANES
    nblk = n // MOBA_BLOCK
    blk = lambda c0: pl.BlockSpec((MOBA_BLOCK, width), lambda i: (i, c0 // h))
    vec = pl.BlockSpec((1, LANES), lambda i: (0, 0))
    return pl.pallas_call(
        functools.partial(_moba_norm_body, nb=nb),
        grid=(nblk,),
        in_specs=[blk(col_q), blk(col_k), blk(col_v), vec, vec],
        out_specs=[blk(0), pl.BlockSpec((MOBA_BLOCK, 2 * width), lambda i: (i, 0)), blk(0),
                   pl.BlockSpec((1, 1, width), lambda i: (i, 0, 0))],
        out_shape=[jax.ShapeDtypeStruct((n, width), BF16), jax.ShapeDtypeStruct((n, 2 * width), BF16),
                   jax.ShapeDtypeStruct((n, width), BF16),
                   jax.ShapeDtypeStruct((nblk, 1, width), F32)],
        compiler_params=_cparams(("parallel",)),
    )(z, z, z, gq.reshape(1, -1), gk.reshape(1, -1))


_MOBA_GROUP = 4


def _moba_attn_body(q_ref, k_ref, v_ref, km_ref, o_ref, *, nb):
    j = pl.program_id(2)
    q = q_ref[...]
    t = q.shape[0]

    km = km_ref[0]
    if nb < LANES:
        km = jnp.concatenate([km, jnp.zeros((LANES - nb, km.shape[1]), F32)], axis=0)
    gate = _dot_nt(q, km.astype(BF16))
    lane = lax.broadcasted_iota(jnp.int32, gate.shape, 1)
    past = lane < j
    masked = jnp.where(past, gate, -jnp.inf)
    s = masked
    kth = None
    for _ in range(MOBA_TOPK):
        kth = jnp.max(s, axis=-1, keepdims=True)
        s = jnp.where(s == kth, -jnp.inf, s)
    attend = (past & (masked >= kth)) | (lane == j)
    q_ext = jnp.concatenate([q, jnp.where(attend, 0.0, NEG_BIG).astype(BF16)], axis=1)

    row = lax.broadcasted_iota(jnp.int32, (t, t), 0)
    col = lax.broadcasted_iota(jnp.int32, (t, t), 1)
    ahead = col - row
    groups = (j + _MOBA_GROUP) // _MOBA_GROUP

    def blocks_of(it):
        return [(groups - 1 - it) * _MOBA_GROUP + c for c in range(_MOBA_GROUP)]

    def scores_of(it):
        return _flash_scores(q_ext, [(k_ref[pl.ds(pl.multiple_of(n * t, t), t), :],
                                      ahead <= jnp.where(n == j, 0, t))
                                     for n in blocks_of(it)])

    def values_of(it):
        return [v_ref[pl.ds(pl.multiple_of(n * t, t), t), :] for n in blocks_of(it)]

    o_ref[...] = _flash_loop(groups, t, v_ref.shape[1], scores_of, values_of)


def _moba_attention(qn, kn, vb, kmean, batch, nb):
    n = qn.shape[0]
    h = MOBA_HEADS
    blk = MOBA_BLOCK
    seq = nb * blk
    return pl.pallas_call(
        functools.partial(_moba_attn_body, nb=nb),
        grid=(batch, h, nb),
        in_specs=[pl.BlockSpec((blk, LANES), lambda b, hh, j: (b * nb + j, hh)),
                  pl.BlockSpec((seq, 2 * LANES), lambda b, hh, j: (b, hh)),
                  pl.BlockSpec((seq, LANES), lambda b, hh, j: (b, hh)),
                  pl.BlockSpec((1, nb, LANES), lambda b, hh, j: (b, 0, hh))],
        out_specs=pl.BlockSpec((blk, LANES), lambda b, hh, j: (b * nb + j, hh)),
        out_shape=jax.ShapeDtypeStruct((n, h * LANES), F32),
        compiler_params=_cparams(("parallel", "parallel", "arbitrary")),
    )(qn, kn, vb, kmean)


_SUBLANES = 8


def _all_sublanes(x, op):
    for shift in (4, 2, 1):
        x = op(x, pltpu.roll(x, shift, axis=0))
    return x


def _bitonic_merge(v):
    v = list(v)
    for j in (8, 4, 2, 1):
        for i in range(16):
            if i & j == 0:
                v[i], v[i | j] = jnp.maximum(v[i], v[i | j]), jnp.minimum(v[i], v[i | j])
    return v


def _top16(v):
    v = list(v)
    k = 2
    while k <= 16:
        j = k // 2
        while j >= 1:
            for i in range(16):
                l = i ^ j
                if l > i:
                    hi, lo = jnp.maximum(v[i], v[l]), jnp.minimum(v[i], v[l])
                    v[i], v[l] = (hi, lo) if i & k == 0 else (lo, hi)
            j //= 2
        k *= 2
    for shift in (4, 2, 1):
        v = _bitonic_merge([jnp.maximum(v[i], pltpu.roll(v[15 - i], shift, axis=0)) for i in range(16)])
    return v


def _next_largest(tiles, kth):
    count = jnp.zeros_like(kth)
    below = jnp.full_like(kth, -jnp.inf)
    for x in tiles:
        count = count + jnp.where(x >= kth, 1.0, 0.0)
        below = jnp.maximum(below, jnp.where(x < kth, x, -jnp.inf))
    count = _all_sublanes(count, jnp.add)
    below = _all_sublanes(below, jnp.maximum)
    return jnp.where(count > PEER_TOPK + 0.5, kth, below)


def _on_sublanes(tiles):
    sub = lax.broadcasted_iota(jnp.int32, tiles[0].shape, 0)
    out = tiles[0]
    for b in range(1, _SUBLANES):
        out = jnp.where(sub == b, tiles[b], out)
    return out


def _peer_route_body(x_ref, g_ref, wq_ref, sk_ref, xnt_ref, thr_ref, s2_ref, e1_ref, e2_ref, q_scr):
    h = pl.program_id(1)

    @pl.when(h == 0)
    def _():
        xn = _rms(x_ref[...], g_ref[...])
        xnt_ref[...] = xn.T.astype(BF16)
        q = _dot(xn.astype(BF16), wq_ref[...])
        for hp in range(2 * PEER_HEADS):
            q_scr[hp] = q[:, hp * LANES:(hp + 1) * LANES].astype(BF16)

    s1_all = _dot_nt(sk_ref[0, 0], q_scr[2 * h])
    s2_all = _dot_nt(sk_ref[0, 1], q_scr[2 * h + 1])
    for lt in range(s1_all.shape[1] // LANES):
        lanes = slice(lt * LANES, (lt + 1) * LANES)
        thr, e1, e2 = _peer_route_tile(s1_all[:, lanes], s2_all[:, lanes])
        thr_ref[0, lt] = thr
        s2_ref[0, lt] = s2_all[:, lanes]
        e1_ref[0, lt] = e1
        e2_ref[0, lt] = e2


def _peer_route_tile(s1, s2):
    tiles1 = [s1[v * _SUBLANES:(v + 1) * _SUBLANES] for v in range(PEER_KEYS // _SUBLANES)]
    tiles2 = [s2[v * _SUBLANES:(v + 1) * _SUBLANES] for v in range(PEER_KEYS // _SUBLANES)]
    top1 = _top16(tiles1)
    top2 = _top16(tiles2)
    next1 = _next_largest(tiles1, top1[-1])
    next2 = _next_largest(tiles2, top2[-1])
    col1 = [_on_sublanes(top1[:_SUBLANES]), _on_sublanes(top1[_SUBLANES:])]
    col2 = [_on_sublanes(top2[:_SUBLANES]), _on_sublanes(top2[_SUBLANES:])]
    cand = [top1[a] + col2[0] for a in range(_SUBLANES)] + [top1[0] + col2[1], top2[0] + col1[1]]
    pad = jnp.full_like(s1[:_SUBLANES], -jnp.inf)
    kth = _top16(cand + [pad] * (16 - len(cand)))[-1]
    nxt = jnp.maximum(_next_largest(cand, kth), jnp.maximum(next1 + top2[0], top1[0] + next2))
    tau = 0.5 * (kth + nxt)
    m1 = top1[0]
    m2 = top2[0]
    z = jnp.zeros_like(tau)
    for c in cand:
        z = z + jnp.where(c >= tau, jnp.exp(c - (m1 + m2)), 0.0)
    inv_z = 1.0 / _all_sublanes(z, jnp.add)
    thr = jnp.concatenate([tau - x for x in tiles1], axis=0)
    e1 = jnp.concatenate([jnp.exp(x - m1) * inv_z for x in tiles1], axis=0)
    e2 = jnp.concatenate([jnp.exp(x - m2) for x in tiles2], axis=0)
    return thr, e1, e2


def _peer_route(x2d, gain, wq_bf16, sk_bf16, t=512):
    n, d = x2d.shape
    hq = wq_bf16.shape[1]
    routed = jax.ShapeDtypeStruct((PEER_HEADS, n // LANES, PEER_KEYS, LANES), F32)
    rspec = pl.BlockSpec((1, t // LANES, PEER_KEYS, LANES), lambda i, h: (h, i, 0, 0))
    return pl.pallas_call(
        _peer_route_body,
        grid=(n // t, PEER_HEADS),
        in_specs=[pl.BlockSpec((t, d), lambda i, h: (i, 0)),
                  pl.BlockSpec((1, d), lambda i, h: (0, 0)),
                  pl.BlockSpec((d, hq), lambda i, h: (0, 0)),
                  pl.BlockSpec((1, 2, PEER_KEYS, LANES), lambda i, h: (h, 0, 0, 0))],
        out_specs=[pl.BlockSpec((d, t), lambda i, h: (0, i)), rspec, rspec, rspec, rspec],
        out_shape=[jax.ShapeDtypeStruct((d, n), BF16), routed, routed, routed, routed],
        scratch_shapes=[pltpu.VMEM((hq // LANES, t, LANES), BF16)],
        compiler_params=_cparams(("parallel", "arbitrary")),
    )(x2d, gain.reshape(1, d), wq_bf16, sk_bf16)


_PEER_ROWS = 2


def _peer_main_body(x_ref, xnt_ref, thr_ref, s2_ref, e1_ref, e2_ref, down_ref, upt_ref, o_ref,
                    acc_ref, act_ref, p_ref, *, bi):
    e = pl.program_id(1)
    t = act_ref.shape[1]
    nj = PEER_KEYS // _SUBLANES
    row0 = pl.multiple_of(e * bi, bi)

    @pl.when(e == 0)
    def _():
        acc_ref[...] = jnp.zeros_like(acc_ref)

    act_ref[...] = _gelu(_dot(down_ref[...], xnt_ref[...]))

    for lt in range(t // LANES):
        lanes = slice(lt * LANES, (lt + 1) * LANES)

        def row_group(g, carry, lt=lt, lanes=lanes):
            i0 = g * _PEER_ROWS
            w = [[jnp.zeros((_SUBLANES, LANES), F32) for _ in range(nj)] for _ in range(_PEER_ROWS)]
            for h in range(PEER_HEADS):
                thr = [jnp.broadcast_to(thr_ref[h, lt, pl.ds(row0 + i0 + r, 1), :], (_SUBLANES, LANES))
                       for r in range(_PEER_ROWS)]
                e1 = [jnp.broadcast_to(e1_ref[h, lt, pl.ds(row0 + i0 + r, 1), :], (_SUBLANES, LANES))
                      for r in range(_PEER_ROWS)]
                for jv in range(nj):
                    rows = slice(jv * _SUBLANES, (jv + 1) * _SUBLANES)
                    s2 = s2_ref[h, lt, rows, :]
                    e2 = e2_ref[h, lt, rows, :]
                    for r in range(_PEER_ROWS):
                        w[r][jv] = w[r][jv] + jnp.where(s2 >= thr[r], e2, 0.0) * e1[r]
            for r in range(_PEER_ROWS):
                for jp in range(nj // 2):
                    start = pl.multiple_of((i0 + r) * PEER_KEYS + jp * 2 * _SUBLANES, 2 * _SUBLANES)
                    rows = pl.ds(start, 2 * _SUBLANES)
                    wt = jnp.concatenate([w[r][2 * jp], w[r][2 * jp + 1]], axis=0)
                    p_ref[rows, lanes] = (wt * act_ref[rows, lanes]).astype(BF16)
            return carry

        lax.fori_loop(0, bi // _PEER_ROWS, row_group, 0)

    acc_ref[...] += _dot(upt_ref[...], p_ref[...])

    @pl.when(e == pl.num_programs(1) - 1)
    def _():
        o_ref[...] = x_ref[...] + acc_ref[...].T


def _peer_main(x2d, xnt, thr, s2, e1, e2, down_bf16, upt_bf16, t=512, bi=8):
    n, d = x2d.shape
    ne = down_bf16.shape[0]
    be = bi * PEER_KEYS
    rspec = pl.BlockSpec((PEER_HEADS, t // LANES, PEER_KEYS, LANES), lambda i, e: (0, i, 0, 0))
    return pl.pallas_call(
        functools.partial(_peer_main_body, bi=bi),
        grid=(n // t, ne // be),
        in_specs=[pl.BlockSpec((t, d), lambda i, e: (i, 0)),
                  pl.BlockSpec((d, t), lambda i, e: (0, i)),
                  rspec, rspec, rspec, rspec,
                  pl.BlockSpec((be, d), lambda i, e: (e, 0)),
                  pl.BlockSpec((d, be), lambda i, e: (0, e))],
        out_specs=pl.BlockSpec((t, d), lambda i, e: (i, 0)),
        out_shape=jax.ShapeDtypeStruct((n, d), F32),
        scratch_shapes=[pltpu.VMEM((d, t), F32), pltpu.VMEM((be, t), F32), pltpu.VMEM((be, t), BF16)],
        compiler_params=_cparams(("parallel", "arbitrary")),
    )(x2d, xnt, thr, s2, e1, e2, down_bf16, upt_bf16)


def _peer_ffn(x2d, gain, w_query, sub_keys, expert_down, expert_up):
    xnt, thr, s2, e1, e2 = _peer_route(x2d, gain, w_query.astype(BF16), sub_keys.astype(BF16))
    return _peer_main(x2d, xnt, thr, s2, e1, e2, expert_down.astype(BF16), expert_up.T.astype(BF16))


def _pad_cols(w, width):
    return jnp.pad(w, ((0, 0), (0, width - w.shape[1])))


def _swap_halves(w):
    half = w.shape[-1] // 2
    return jnp.concatenate([w[..., half:], w[..., :half]], axis=-1)


def _rope_gains(g):
    rope = g[MLA_NOPE:]
    return jnp.stack([g[:MLA_NOPE],
                      jnp.pad(rope, (0, LANES - MLA_ROPE)),
                      jnp.pad(_swap_halves(rope), (0, LANES - MLA_ROPE))])


def kernel(x, positions, lb_logits, l0_norm_mix, l0_w_in, l0_hgrn_out_norm, l0_gmlp_v_norm, l0_gmlp_w_s, l0_gmlp_b_s, l0_w_out, l0_norm_ffn, l0_peer_w_query, l0_peer_sub_keys, l0_peer_expert_down, l0_peer_expert_up, l1_norm_mix, l1_w_in, l1_mla_cq_norm, l1_mla_ckv_norm, l1_mla_w_uq, l1_mla_w_ukv, l1_mla_q_norm, l1_mla_k_norm, l1_moba_q_norm, l1_moba_k_norm, l1_w_out, l1_norm_ffn, l1_peer_w_query, l1_peer_sub_keys, l1_peer_expert_down, l1_peer_expert_up):
    batch, seq, d = x.shape
    n = batch * seq
    x2d = x.reshape(n, d)

    lb0 = jax.nn.softmax(lb_logits.astype(F32), axis=0)[0]
    z0 = _norm_matmul(x2d, l0_norm_mix, l0_w_in.astype(BF16))
    out_a, out_b = _l0_mixer(z0, lb0, l0_hgrn_out_norm, l0_gmlp_v_norm, l0_gmlp_w_s, l0_gmlp_b_s, batch, seq)
    half = out_a.shape[1]
    w_out0 = l0_w_out.astype(BF16)
    x2d = _out_proj(x2d, out_a, out_b, w_out0[:half], w_out0[half:])
    x2d = _peer_ffn(x2d, l0_norm_ffn, l0_peer_w_query, l0_peer_sub_keys, l0_peer_expert_down, l0_peer_expert_up)

    q_lora, kv_lora = l1_mla_cq_norm.shape[0], l1_mla_ckv_norm.shape[0]
    c0 = q_lora + kv_lora
    dw = MOBA_HEADS * LANES
    w1 = l1_w_in
    k_pe_w = w1[:, c0:c0 + MLA_ROPE]
    moba_w = w1[:, c0 + MLA_ROPE:]
    w1p = jnp.concatenate([moba_w, w1[:, :c0], _pad_cols(k_pe_w, LANES),
                           _pad_cols(_swap_halves(k_pe_w), LANES)], axis=1).astype(BF16)
    z1 = _norm_matmul(x2d, l1_norm_mix, w1p)

    half_r = MLA_ROPE // 2
    inv_freq = ROPE_THETA ** (-jnp.arange(half_r, dtype=F32) / half_r)
    ang = positions.reshape(n, 1).astype(F32) * inv_freq
    zeros = jnp.zeros((n, LANES - MLA_ROPE), F32)
    cos_t = jnp.concatenate([jnp.cos(ang), jnp.cos(ang), zeros], axis=1)
    sin_t = jnp.concatenate([-jnp.sin(ang), jnp.sin(ang), zeros], axis=1)

    wuq = l1_mla_w_uq.reshape(q_lora, MLA_HEADS, MLA_QK)
    wuq_p = jnp.concatenate(
        [wuq[..., :MLA_NOPE],
         jnp.pad(wuq[..., MLA_NOPE:], ((0, 0), (0, 0), (0, LANES - MLA_ROPE))),
         jnp.pad(_swap_halves(wuq[..., MLA_NOPE:]), ((0, 0), (0, 0), (0, LANES - MLA_ROPE)))],
        axis=-1).reshape(q_lora, MLA_HEADS * 3 * LANES).astype(BF16)
    q_c, k_c, v_c = _mla_prep(z1, cos_t, sin_t, l1_mla_cq_norm.reshape(1, -1), l1_mla_ckv_norm.reshape(1, -1),
                              wuq_p, l1_mla_w_ukv.astype(BF16), _rope_gains(l1_mla_q_norm),
                              _rope_gains(l1_mla_k_norm), col0=3 * MOBA_HEADS)
    out_c = _mla_attention(q_c, k_c, v_c, batch, seq)

    nb = seq // MOBA_BLOCK
    qn, kn, vb, kmean = _moba_norm(z1, l1_moba_q_norm, l1_moba_k_norm,
                                   col_q=0, col_k=MOBA_HEADS, col_v=2 * MOBA_HEADS, nb=nb)
    out_d = _moba_attention(qn, kn, vb, kmean.reshape(batch, nb, dw), batch, nb)

    w_out1 = l1_w_out.astype(BF16)
    x2d = _out_proj(x2d, out_c, out_d, w_out1[:out_c.shape[1]], w_out1[out_c.shape[1]:])
    x2d = _peer_ffn(x2d, l1_norm_ffn, l1_peer_w_query, l1_peer_sub_keys, l1_peer_expert_down, l1_peer_expert_up)
    return x2d.reshape(batch, seq, d)
```

```python
import functools

import numpy as np
import jax
import jax.numpy as jnp
from jax import lax
from jax.experimental import pallas as pl
from jax.experimental.pallas import tpu as pltpu

F32 = jnp.float32
BF16 = jnp.bfloat16

RMS_EPS = 1e-6
LANES = 128
VMEM_LIMIT = 56 * 1024 * 1024

HGRN_HEADS = 4
GMLP_GROUPS = 4
MIX_CHUNK = 128
MLA_HEADS = 4
MLA_NOPE = 128
MLA_ROPE = 64
MLA_QK = MLA_NOPE + MLA_ROPE
ROPE_THETA = 10000.0
MOBA_HEADS = 4
MOBA_BLOCK = 256
MOBA_TOPK = 3
PEER_HEADS = 8
PEER_KEYS = 128
PEER_TOPK = 16
NEG_BIG = -1e30


def _cparams(semantics):
    return pltpu.CompilerParams(dimension_semantics=semantics, vmem_limit_bytes=VMEM_LIMIT)


def _dot(a, b):
    return jnp.dot(a, b, preferred_element_type=F32)


def _dot_nt(a, b):
    return lax.dot_general(a, b, (((1,), (1,)), ((), ())), preferred_element_type=F32)


def _dot_tn(a, b):
    return lax.dot_general(a, b, (((0,), (0,)), ((), ())), preferred_element_type=F32)


def _rms(x, gain):
    return x * lax.rsqrt(jnp.mean(x * x, axis=-1, keepdims=True) + RMS_EPS) * gain


def _gelu(x):
    c = np.float32(np.sqrt(2.0 / np.pi))
    half = 0.5 * x
    return half + half * jnp.tanh(x * (c + np.float32(c * 0.044715) * (x * x)))


def _silu(x):
    return x * jax.nn.sigmoid(x)


def _norm_matmul_body(x_ref, g_ref, w_ref, o_ref):
    y = _rms(x_ref[...], g_ref[...])
    o_ref[...] = _dot(y.astype(BF16), w_ref[...])


def _norm_matmul(x2d, gain, w_bf16, tm=512):
    n, d = x2d.shape
    m = w_bf16.shape[1]
    return pl.pallas_call(
        _norm_matmul_body,
        grid=(n // tm,),
        in_specs=[pl.BlockSpec((tm, d), lambda i: (i, 0)),
                  pl.BlockSpec((1, d), lambda i: (0, 0)),
                  pl.BlockSpec((d, m), lambda i: (0, 0))],
        out_specs=pl.BlockSpec((tm, m), lambda i: (i, 0)),
        out_shape=jax.ShapeDtypeStruct((n, m), F32),
        compiler_params=_cparams(("parallel",)),
    )(x2d, gain.reshape(1, d), w_bf16)


def _out_proj_body(x_ref, a_ref, b_ref, wa_ref, wb_ref, o_ref):
    o_ref[...] = (x_ref[...] + _dot(a_ref[...].astype(BF16), wa_ref[...])
                  + _dot(b_ref[...].astype(BF16), wb_ref[...]))


def _out_proj(x2d, a, b, wa_bf16, wb_bf16, tm=512):
    n, d = x2d.shape
    ka, kb = a.shape[1], b.shape[1]
    return pl.pallas_call(
        _out_proj_body,
        grid=(n // tm,),
        in_specs=[pl.BlockSpec((tm, d), lambda i: (i, 0)),
                  pl.BlockSpec((tm, ka), lambda i: (i, 0)),
                  pl.BlockSpec((tm, kb), lambda i: (i, 0)),
                  pl.BlockSpec((ka, d), lambda i: (0, 0)),
                  pl.BlockSpec((kb, d), lambda i: (0, 0))],
        out_specs=pl.BlockSpec((tm, d), lambda i: (i, 0)),
        out_shape=jax.ShapeDtypeStruct((n, d), F32),
        compiler_params=_cparams(("parallel",)),
    )(x2d, a, b, wa_bf16, wb_bf16)


_HGRN_LEVELS = (64, 32, 16, 8, 4, 2, 1)


def _hgrn_exponent_matrix(c):
    a = np.zeros((len(_HGRN_LEVELS) + 1, c, c), np.float32)
    r = np.arange(c)[:, None]
    u = np.arange(c)[None, :]
    a[0] = (u <= r)
    for li, w in enumerate(_HGRN_LEVELS):
        p = (r // (2 * w)) * (2 * w) + w
        second = r >= p
        a[li + 1] = np.where(second, (u >= p) & (u <= r), (u > r) & (u <= p - 1))
    return a.reshape(-1, c)


_L0_HEADS_PER_STEP = 2


def _hgrn_head(q, fg, inp, og, lb, out_gain, amat, state_t):
    c = MIX_CHUNK
    row = lax.broadcasted_iota(jnp.int32, (c, c), 0)
    col = lax.broadcasted_iota(jnp.int32, (c, c), 1)
    rowc = lax.broadcasted_iota(jnp.int32, (c, 1), 0)
    f = lb + (1.0 - lb) * jax.nn.sigmoid(fg)
    k = 1.0 - f
    g = jnp.log(f)
    val = _silu(inp)
    g1 = g.astype(BF16)
    r1 = g - g1.astype(F32)
    g2 = r1.astype(BF16)
    g3 = (r1 - g2.astype(F32)).astype(BF16)
    dall = _dot(amat, g1) + _dot(amat, g2) + _dot(amat, g3)
    b = dall[0:c]

    scores = jnp.where(row == col, _dot_nt(q.astype(BF16), k.astype(BF16)), 0.0)
    for li, w in enumerate(_HGRN_LEVELS):
        e = jnp.exp(dall[(li + 1) * c:(li + 2) * c])
        sh = int(np.log2(w))
        second = (jnp.right_shift(rowc, sh) & 1) == 1
        qt = jnp.where(second, q * e, 0.0).astype(BF16)
        kt = jnp.where(second, 0.0, k * e).astype(BF16)
        same = jnp.right_shift(row, sh + 1) == jnp.right_shift(col, sh + 1)
        scores = scores + jnp.where(same, _dot_nt(qt, kt), 0.0)
    val16 = val.astype(BF16)
    o = _dot(scores.astype(BF16), val16)
    o = o + _dot_nt((q * jnp.exp(b)).astype(BF16), state_t.astype(BF16))
    b_last = b[c - 1:c]
    k_dec = (k * jnp.exp(b_last - b)).astype(BF16)
    new_state = state_t * jnp.exp(b_last) + _dot_tn(val16, k_dec)
    return _rms(o, out_gain) * _silu(og), new_state


def _gmlp_group(u, v, v_gain, w_s, b_s):
    c = MIX_CHUNK
    row = lax.broadcasted_iota(jnp.int32, (c, c), 0)
    col = lax.broadcasted_iota(jnp.int32, (c, c), 1)
    vn = _rms(_gelu(v), v_gain)
    w_causal = jnp.where(row >= col, w_s, 0.0).astype(BF16)
    return _gelu(u) * (_dot(w_causal, vn.astype(BF16)) + b_s)


def _l0_mix_body(q_ref, fg_ref, inp_ref, og_ref, u_ref, v_ref, lb_ref, on_ref, vn_ref,
                 ws_ref, bs_ref, amat_ref, oa_ref, ob_ref, state_ref):
    @pl.when(pl.program_id(2) == 0)
    def _():
        state_ref[...] = jnp.zeros_like(state_ref)

    amat = amat_ref[...]
    for p in range(_L0_HEADS_PER_STEP):
        lanes = slice(p * LANES, (p + 1) * LANES)
        oa, new_state = _hgrn_head(q_ref[:, lanes], fg_ref[:, lanes], inp_ref[:, lanes], og_ref[:, lanes],
                                   lb_ref[:, lanes], on_ref[:, lanes], amat, state_ref[p])
        state_ref[p] = new_state
        oa_ref[:, lanes] = oa
        ob_ref[:, lanes] = _gmlp_group(u_ref[:, lanes], v_ref[:, lanes], vn_ref[:, lanes], ws_ref[p], bs_ref[p])


def _l0_mixer(z, lb, out_norm, v_norm, w_s, b_s, batch, seq):
    n = z.shape[0]
    c = MIX_CHUNK
    nc = seq // c
    hp = _L0_HEADS_PER_STEP
    width = hp * LANES
    steps = HGRN_HEADS // hp
    amat = jnp.asarray(_hgrn_exponent_matrix(c), BF16)
    na = amat.shape[0]

    def zcol(group):
        return pl.BlockSpec((c, width), lambda b, hh, i, g=group: (b * nc + i, g * steps + hh))

    def vec():
        return pl.BlockSpec((1, width), lambda b, hh, i: (0, hh))

    out_spec = pl.BlockSpec((c, width), lambda b, hh, i: (b * nc + i, hh))
    return pl.pallas_call(
        _l0_mix_body,
        grid=(batch, steps, nc),
        in_specs=[zcol(0), zcol(1), zcol(2), zcol(3), zcol(4), zcol(5),
                  vec(), vec(), vec(),
                  pl.BlockSpec((hp, c, c), lambda b, hh, i: (hh, 0, 0)),
                  pl.BlockSpec((hp, c, 1), lambda b, hh, i: (hh, 0, 0)),
                  pl.BlockSpec((na, c), lambda b, hh, i: (0, 0))],
        out_specs=[out_spec, out_spec],
        out_shape=[jax.ShapeDtypeStruct((n, HGRN_HEADS * LANES), F32),
                   jax.ShapeDtypeStruct((n, HGRN_HEADS * LANES), F32)],
        scratch_shapes=[pltpu.VMEM((hp, LANES, LANES), F32)],
        compiler_params=_cparams(("parallel", "parallel", "arbitrary")),
    )(z, z, z, z, z, z, lb.reshape(1, -1), out_norm.reshape(1, -1), v_norm.reshape(1, -1),
      w_s, b_s.reshape(GMLP_GROUPS, c, 1), amat)


def _mla_prep_body(cq_ref, ckv_ref, kpe_ref, kpes_ref, cos_ref, sin_ref, cqn_ref, ckvn_ref,
                   wuq_ref, wukv_ref, gq_ref, gk_ref, q_out, k_out, v_out):
    cos = cos_ref[...]
    sin = sin_ref[...]
    qn = _rms(cq_ref[...], cqn_ref[...])
    kvn = _rms(ckv_ref[...], ckvn_ref[...])
    qall = _dot(qn.astype(BF16), wuq_ref[...])
    kvall = _dot(kvn.astype(BF16), wukv_ref[...])
    kpe = kpe_ref[...]
    kpes = kpes_ref[...]
    kpe_sq = jnp.sum(kpe * kpe, axis=-1, keepdims=True)
    gq = gq_ref[...]
    gk = gk_ref[...]
    scale = np.float32(MLA_QK ** -0.5)
    for h in range(MLA_HEADS):
        nope = qall[:, 3 * h * LANES:(3 * h + 1) * LANES]
        rp = qall[:, (3 * h + 1) * LANES:(3 * h + 2) * LANES]
        rps = qall[:, (3 * h + 2) * LANES:(3 * h + 3) * LANES]
        ms = (jnp.sum(nope * nope, axis=-1, keepdims=True)
              + jnp.sum(rp * rp, axis=-1, keepdims=True)) * np.float32(1.0 / MLA_QK)
        r = lax.rsqrt(ms + RMS_EPS) * scale
        q_out[:, 2 * h * LANES:(2 * h + 1) * LANES] = (nope * r * gq[0:1]).astype(BF16)
        q_out[:, (2 * h + 1) * LANES:(2 * h + 2) * LANES] = (
            (rp * r * gq[1:2]) * cos + (rps * r * gq[2:3]) * sin).astype(BF16)
        knope = kvall[:, 2 * h * LANES:(2 * h + 1) * LANES]
        ms = (jnp.sum(knope * knope, axis=-1, keepdims=True) + kpe_sq) * np.float32(1.0 / MLA_QK)
        r = lax.rsqrt(ms + RMS_EPS)
        k_out[:, 2 * h * LANES:(2 * h + 1) * LANES] = (knope * r * gk[0:1]).astype(BF16)
        k_out[:, (2 * h + 1) * LANES:(2 * h + 2) * LANES] = (
            (kpe * r * gk[1:2]) * cos + (kpes * r * gk[2:3]) * sin).astype(BF16)
        v_out[:, h * LANES:(h + 1) * LANES] = kvall[:, (2 * h + 1) * LANES:(2 * h + 2) * LANES].astype(BF16)


def _mla_prep(z, cos, sin, cq_norm, ckv_norm, wuq_p, wukv, gq, gk, col0, tm=256):
    n = z.shape[0]
    hq = MLA_HEADS * 2 * LANES
    full = lambda a: pl.BlockSpec(a.shape, lambda i: (0,) * a.ndim)
    return pl.pallas_call(
        _mla_prep_body,
        grid=(n // tm,),
        in_specs=[pl.BlockSpec((tm, 2 * LANES), lambda i: (i, col0 // 2)),
                  pl.BlockSpec((tm, LANES), lambda i: (i, col0 + 2)),
                  pl.BlockSpec((tm, LANES), lambda i: (i, col0 + 3)),
                  pl.BlockSpec((tm, LANES), lambda i: (i, col0 + 4)),
                  pl.BlockSpec((tm, LANES), lambda i: (i, 0)),
                  pl.BlockSpec((tm, LANES), lambda i: (i, 0)),
                  full(cq_norm), full(ckv_norm), full(wuq_p), full(wukv), full(gq), full(gk)],
        out_specs=[pl.BlockSpec((tm, hq), lambda i: (i, 0)),
                   pl.BlockSpec((tm, hq), lambda i: (i, 0)),
                   pl.BlockSpec((tm, MLA_HEADS * LANES), lambda i: (i, 0))],
        out_shape=[jax.ShapeDtypeStruct((n, hq), BF16),
                   jax.ShapeDtypeStruct((n, hq), BF16),
                   jax.ShapeDtypeStruct((n, MLA_HEADS * LANES), BF16)],
        compiler_params=_cparams(("parallel",)),
    )(z, z, z, z, cos, sin, cq_norm, ckv_norm, wuq_p, wukv, gq, gk)


def _flash_scores(q, keys_masks):
    return [jnp.where(mask, _dot_nt(q, k), NEG_BIG) for k, mask in keys_masks]


def _flash_apply(state, scores, values):
    m_prev, l_prev, acc = state
    m_new = m_prev
    for s in scores:
        m_new = jnp.maximum(m_new, jnp.max(s, axis=-1, keepdims=True))
    alpha = jnp.exp(m_prev - m_new)
    l_new = alpha * l_prev
    acc = alpha * acc
    for s, v in zip(scores, values):
        p = jnp.exp(s - m_new)
        l_new = l_new + jnp.sum(p, axis=-1, keepdims=True)
        acc = acc + _dot(p.astype(BF16), v)
    return m_new, l_new, acc


def _flash_loop(steps, tq, dv, scores_of, values_of):
    state = (jnp.full((tq, 1), NEG_BIG, F32), jnp.zeros((tq, 1), F32), jnp.zeros((tq, dv), F32))
    state = lax.fori_loop(0, steps, lambda i, st: _flash_apply(st, scores_of(i), values_of(i)), state)
    return state[2] / state[1]


def _mla_attn_body(q_ref, k_ref, v_ref, o_ref, *, tk):
    qi = pl.program_id(2)
    q = q_ref[...]
    tq = q.shape[0]
    row = lax.broadcasted_iota(jnp.int32, (tq, tk), 0)
    col = lax.broadcasted_iota(jnp.int32, (tq, tk), 1)
    ahead = col - row

    def scores_of(n):
        start = pl.multiple_of(n * tk, tk)
        mask = ahead <= qi * tq - n * tk
        return _flash_scores(q, [(k_ref[pl.ds(start, tk), :], mask)])

    def values_of(n):
        return [v_ref[pl.ds(pl.multiple_of(n * tk, tk), tk), :]]

    o_ref[...] = _flash_loop((qi + 1) * (tq // tk), tq, v_ref.shape[1], scores_of, values_of)


def _mla_attention(q, k, v, batch, seq, tq=512, tk=512):
    n = q.shape[0]
    tq = min(tq, seq)
    nt = seq // tq
    return pl.pallas_call(
        functools.partial(_mla_attn_body, tk=tk),
        grid=(batch, MLA_HEADS, nt),
        in_specs=[pl.BlockSpec((tq, 2 * LANES), lambda b, h, qi: (b * nt + qi, h)),
                  pl.BlockSpec((seq, 2 * LANES), lambda b, h, qi: (b, h)),
                  pl.BlockSpec((seq, LANES), lambda b, h, qi: (b, h))],
        out_specs=pl.BlockSpec((tq, LANES), lambda b, h, qi: (b * nt + qi, h)),
        out_shape=jax.ShapeDtypeStruct((n, MLA_HEADS * LANES), F32),
        compiler_params=_cparams(("parallel", "parallel", "arbitrary")),
    )(q, k, v)


def _moba_norm_body(q_ref, k_ref, v_ref, gq_ref, gk_ref, qn_ref, kn_ref, vb_ref, km_ref, *, nb):
    block_id = pl.program_id(0) % nb
    lane = lax.broadcasted_iota(jnp.int32, (q_ref.shape[0], LANES), 1)
    one_hot = jnp.where(lane == block_id, 1.0, 0.0).astype(BF16)
    vb_ref[...] = v_ref[...].astype(BF16)
    for h in range(MOBA_HEADS):
        lanes = slice(h * LANES, (h + 1) * LANES)
        qn_ref[:, lanes] = (_rms(q_ref[:, lanes], gq_ref[...]) * np.float32(LANES ** -0.5)).astype(BF16)
        kn = _rms(k_ref[:, lanes], gk_ref[...])
        kn_ref[:, 2 * h * LANES:(2 * h + 1) * LANES] = kn.astype(BF16)
        kn_ref[:, (2 * h + 1) * LANES:(2 * h + 2) * LANES] = one_hot
        km_ref[0, :, lanes] = jnp.mean(kn, axis=0, keepdims=True)


def _moba_norm(z, gq, gk, col_q, col_k, col_v, nb):
    n = z.shape[0]
    h = MOBA_HEADS
    width = h * LANES
    nblk = n // MOBA_BLOCK
    blk = lambda c0: pl.BlockSpec((MOBA_BLOCK, width), lambda i: (i, c0 // h))
    vec = pl.BlockSpec((1, LANES), lambda i: (0, 0))
    return pl.pallas_call(
        functools.partial(_moba_norm_body, nb=nb),
        grid=(nblk,),
        in_specs=[blk(col_q), blk(col_k), blk(col_v), vec, vec],
        out_specs=[blk(0), pl.BlockSpec((MOBA_BLOCK, 2 * width), lambda i: (i, 0)), blk(0),
                   pl.BlockSpec((1, 1, width), lambda i: (i, 0, 0))],
        out_shape=[jax.ShapeDtypeStruct((n, width), BF16), jax.ShapeDtypeStruct((n, 2 * width), BF16),
                   jax.ShapeDtypeStruct((n, width), BF16),
                   jax.ShapeDtypeStruct((nblk, 1, width), F32)],
        compiler_params=_cparams(("parallel",)),
    )(z, z, z, gq.reshape(1, -1), gk.reshape(1, -1))


_MOBA_GROUP = 8


def _moba_attn_body(q_ref, k_ref, v_ref, km_ref, o_ref, *, nb):
    j = pl.program_id(2)
    q = q_ref[...]
    t = q.shape[0]

    km = km_ref[0]
    if nb < LANES:
        km = jnp.concatenate([km, jnp.zeros((LANES - nb, km.shape[1]), F32)], axis=0)
    gate = _dot_nt(q, km.astype(BF16))
    lane = lax.broadcasted_iota(jnp.int32, gate.shape, 1)
    past = lane < j
    masked = jnp.where(past, gate, -jnp.inf)
    s = masked
    kth = None
    for _ in range(MOBA_TOPK):
        kth = jnp.max(s, axis=-1, keepdims=True)
        s = jnp.where(s == kth, -jnp.inf, s)
    attend = (past & (masked >= kth)) | (lane == j)
    q_ext = jnp.concatenate([q, jnp.where(attend, 0.0, NEG_BIG).astype(BF16)], axis=1)

    row = lax.broadcasted_iota(jnp.int32, (t, t), 0)
    col = lax.broadcasted_iota(jnp.int32, (t, t), 1)
    ahead = col - row
    groups = (j + _MOBA_GROUP) // _MOBA_GROUP

    def blocks_of(it):
        return [(groups - 1 - it) * _MOBA_GROUP + c for c in range(_MOBA_GROUP)]

    def scores_of(it):
        return _flash_scores(q_ext, [(k_ref[pl.ds(pl.multiple_of(n * t, t), t), :],
                                      ahead <= jnp.where(n == j, 0, t))
                                     for n in blocks_of(it)])

    def values_of(it):
        return [v_ref[pl.ds(pl.multiple_of(n * t, t), t), :] for n in blocks_of(it)]

    o_ref[...] = _flash_loop(groups, t, v_ref.shape[1], scores_of, values_of)


def _moba_attention(qn, kn, vb, kmean, batch, nb):
    n = qn.shape[0]
    h = MOBA_HEADS
    blk = MOBA_BLOCK
    seq = nb * blk
    assert nb % _MOBA_GROUP == 0, "the grouped block loop reads whole groups of resident key blocks"
    return pl.pallas_call(
        functools.partial(_moba_attn_body, nb=nb),
        grid=(batch, h, nb),
        in_specs=[pl.BlockSpec((blk, LANES), lambda b, hh, j: (b * nb + j, hh)),
                  pl.BlockSpec((seq, 2 * LANES), lambda b, hh, j: (b, hh)),
                  pl.BlockSpec((seq, LANES), lambda b, hh, j: (b, hh)),
                  pl.BlockSpec((1, nb, LANES), lambda b, hh, j: (b, 0, hh))],
        out_specs=pl.BlockSpec((blk, LANES), lambda b, hh, j: (b * nb + j, hh)),
        out_shape=jax.ShapeDtypeStruct((n, h * LANES), F32),
        compiler_params=_cparams(("parallel", "parallel", "arbitrary")),
    )(qn, kn, vb, kmean)


_SUBLANES = 8


def _all_sublanes(x, op):
    for shift in (4, 2, 1):
        x = op(x, pltpu.roll(x, shift, axis=0))
    return x


def _bitonic_merge(v):
    v = list(v)
    for j in (8, 4, 2, 1):
        for i in range(16):
            if i & j == 0:
                v[i], v[i | j] = jnp.maximum(v[i], v[i | j]), jnp.minimum(v[i], v[i | j])
    return v


def _top16(v):
    v = list(v)
    k = 2
    while k <= 16:
        j = k // 2
        while j >= 1:
            for i in range(16):
                l = i ^ j
                if l > i:
                    hi, lo = jnp.maximum(v[i], v[l]), jnp.minimum(v[i], v[l])
                    v[i], v[l] = (hi, lo) if i & k == 0 else (lo, hi)
            j //= 2
        k *= 2
    for shift in (4, 2, 1):
        v = _bitonic_merge([jnp.maximum(v[i], pltpu.roll(v[15 - i], shift, axis=0)) for i in range(16)])
    return v


def _next_largest(tiles, kth):
    count = jnp.zeros_like(kth)
    below = jnp.full_like(kth, -jnp.inf)
    for x in tiles:
        count = count + jnp.where(x >= kth, 1.0, 0.0)
        below = jnp.maximum(below, jnp.where(x < kth, x, -jnp.inf))
    count = _all_sublanes(count, jnp.add)
    below = _all_sublanes(below, jnp.maximum)
    return jnp.where(count > PEER_TOPK + 0.5, kth, below)


def _on_sublanes(tiles):
    sub = lax.broadcasted_iota(jnp.int32, tiles[0].shape, 0)
    out = tiles[0]
    for b in range(1, _SUBLANES):
        out = jnp.where(sub == b, tiles[b], out)
    return out


def _peer_route_body(x_ref, g_ref, wq_ref, sk_ref, xnt_ref, thr_ref, s2_ref, e1_ref, e2_ref, q_scr):
    h = pl.program_id(1)

    @pl.when(h == 0)
    def _():
        xn = _rms(x_ref[...], g_ref[...])
        xnt_ref[...] = xn.T.astype(BF16)
        q = _dot(xn.astype(BF16), wq_ref[...])
        for hp in range(2 * PEER_HEADS):
            q_scr[hp] = q[:, hp * LANES:(hp + 1) * LANES].astype(BF16)

    s1_all = _dot_nt(sk_ref[0, 0], q_scr[2 * h])
    s2_all = _dot_nt(sk_ref[0, 1], q_scr[2 * h + 1])
    for lt in range(s1_all.shape[1] // LANES):
        lanes = slice(lt * LANES, (lt + 1) * LANES)
        thr, e1, e2 = _peer_route_tile(s1_all[:, lanes], s2_all[:, lanes])
        thr_ref[0, lt] = thr
        s2_ref[0, lt] = s2_all[:, lanes]
        e1_ref[0, lt] = e1
        e2_ref[0, lt] = e2


def _peer_route_tile(s1, s2):
    tiles1 = [s1[v * _SUBLANES:(v + 1) * _SUBLANES] for v in range(PEER_KEYS // _SUBLANES)]
    tiles2 = [s2[v * _SUBLANES:(v + 1) * _SUBLANES] for v in range(PEER_KEYS // _SUBLANES)]
    top1 = _top16(tiles1)
    top2 = _top16(tiles2)
    next1 = _next_largest(tiles1, top1[-1])
    next2 = _next_largest(tiles2, top2[-1])
    col1 = [_on_sublanes(top1[:_SUBLANES]), _on_sublanes(top1[_SUBLANES:])]
    col2 = [_on_sublanes(top2[:_SUBLANES]), _on_sublanes(top2[_SUBLANES:])]
    cand = [top1[a] + col2[0] for a in range(_SUBLANES)] + [top1[0] + col2[1], top2[0] + col1[1]]
    pad = jnp.full_like(s1[:_SUBLANES], -jnp.inf)
    kth = _top16(cand + [pad] * (16 - len(cand)))[-1]
    nxt = jnp.maximum(_next_largest(cand, kth), jnp.maximum(next1 + top2[0], top1[0] + next2))
    tau = 0.5 * (kth + nxt)
    m1 = top1[0]
    m2 = top2[0]
    z = jnp.zeros_like(tau)
    for c in cand:
        z = z + jnp.where(c >= tau, jnp.exp(c - (m1 + m2)), 0.0)
    inv_z = 1.0 / _all_sublanes(z, jnp.add)
    thr = jnp.concatenate([tau - x for x in tiles1], axis=0)
    e1 = jnp.concatenate([jnp.exp(x - m1) * inv_z for x in tiles1], axis=0)
    e2 = jnp.concatenate([jnp.exp(x - m2) for x in tiles2], axis=0)
    return thr, e1, e2


def _peer_route(x2d, gain, wq_bf16, sk_bf16, t=512):
    n, d = x2d.shape
    hq = wq_bf16.shape[1]
    routed = jax.ShapeDtypeStruct((PEER_HEADS, n // LANES, PEER_KEYS, LANES), F32)
    rspec = pl.BlockSpec((1, t // LANES, PEER_KEYS, LANES), lambda i, h: (h, i, 0, 0))
    return pl.pallas_call(
        _peer_route_body,
        grid=(n // t, PEER_HEADS),
        in_specs=[pl.BlockSpec((t, d), lambda i, h: (i, 0)),
                  pl.BlockSpec((1, d), lambda i, h: (0, 0)),
                  pl.BlockSpec((d, hq), lambda i, h: (0, 0)),
                  pl.BlockSpec((1, 2, PEER_KEYS, LANES), lambda i, h: (h, 0, 0, 0))],
        out_specs=[pl.BlockSpec((d, t), lambda i, h: (0, i)), rspec, rspec, rspec, rspec],
        out_shape=[jax.ShapeDtypeStruct((d, n), BF16), routed, routed, routed, routed],
        scratch_shapes=[pltpu.VMEM((hq // LANES, t, LANES), BF16)],
        compiler_params=_cparams(("parallel", "arbitrary")),
    )(x2d, gain.reshape(1, d), wq_bf16, sk_bf16)


_PEER_ROWS = 2


def _peer_main_body(x_ref, xnt_ref, thr_ref, s2_ref, e1_ref, e2_ref, down_ref, upt_ref, o_ref,
                    acc_ref, act_ref, p_ref, *, bi):
    e = pl.program_id(1)
    t = act_ref.shape[1]
    nj = PEER_KEYS // _SUBLANES
    row0 = pl.multiple_of(e * bi, bi)

    @pl.when(e == 0)
    def _():
        acc_ref[...] = jnp.zeros_like(acc_ref)

    act_ref[...] = _gelu(_dot(down_ref[...], xnt_ref[...]))

    for lt in range(t // LANES):
        lanes = slice(lt * LANES, (lt + 1) * LANES)

        def row_group(g, carry, lt=lt, lanes=lanes):
            i0 = g * _PEER_ROWS
            w = [[jnp.zeros((_SUBLANES, LANES), F32) for _ in range(nj)] for _ in range(_PEER_ROWS)]
            for h in range(PEER_HEADS):
                thr = [jnp.broadcast_to(thr_ref[h, lt, pl.ds(row0 + i0 + r, 1), :], (_SUBLANES, LANES))
                       for r in range(_PEER_ROWS)]
                e1 = [jnp.broadcast_to(e1_ref[h, lt, pl.ds(row0 + i0 + r, 1), :], (_SUBLANES, LANES))
                      for r in range(_PEER_ROWS)]
                for jv in range(nj):
                    rows = slice(jv * _SUBLANES, (jv + 1) * _SUBLANES)
                    s2 = s2_ref[h, lt, rows, :]
                    e2 = e2_ref[h, lt, rows, :]
                    for r in range(_PEER_ROWS):
                        w[r][jv] = w[r][jv] + jnp.where(s2 >= thr[r], e2, 0.0) * e1[r]
            for r in range(_PEER_ROWS):
                for jp in range(nj // 2):
                    start = pl.multiple_of((i0 + r) * PEER_KEYS + jp * 2 * _SUBLANES, 2 * _SUBLANES)
                    rows = pl.ds(start, 2 * _SUBLANES)
                    wt = jnp.concatenate([w[r][2 * jp], w[r][2 * jp + 1]], axis=0)
                    p_ref[rows, lanes] = (wt * act_ref[rows, lanes]).astype(BF16)
            return carry

        lax.fori_loop(0, bi // _PEER_ROWS, row_group, 0)

    acc_ref[...] += _dot(upt_ref[...], p_ref[...])

    @pl.when(e == pl.num_programs(1) - 1)
    def _():
        o_ref[...] = x_ref[...] + acc_ref[...].T


def _peer_main(x2d, xnt, thr, s2, e1, e2, down_bf16, upt_bf16, t=512, bi=16):
    n, d = x2d.shape
    ne = down_bf16.shape[0]
    be = bi * PEER_KEYS
    rspec = pl.BlockSpec((PEER_HEADS, t // LANES, PEER_KEYS, LANES), lambda i, e: (0, i, 0, 0))
    return pl.pallas_call(
        functools.partial(_peer_main_body, bi=bi),
        grid=(n // t, ne // be),
        in_specs=[pl.BlockSpec((t, d), lambda i, e: (i, 0)),
                  pl.BlockSpec((d, t), lambda i, e: (0, i)),
                  rspec, rspec, rspec, rspec,
                  pl.BlockSpec((be, d), lambda i, e: (e, 0)),
                  pl.BlockSpec((d, be), lambda i, e: (0, e))],
        out_specs=pl.BlockSpec((t, d), lambda i, e: (i, 0)),
        out_shape=jax.ShapeDtypeStruct((n, d), F32),
        scratch_shapes=[pltpu.VMEM((d, t), F32), pltpu.VMEM((be, t), F32), pltpu.VMEM((be, t), BF16)],
        compiler_params=_cparams(("parallel", "arbitrary")),
    )(x2d, xnt, thr, s2, e1, e2, down_bf16, upt_bf16)


def _peer_ffn(x2d, gain, w_query, sub_keys, expert_down, expert_up):
    xnt, thr, s2, e1, e2 = _peer_route(x2d, gain, w_query.astype(BF16), sub_keys.astype(BF16))
    return _peer_main(x2d, xnt, thr, s2, e1, e2, expert_down.astype(BF16), expert_up.T.astype(BF16))


def _pad_cols(w, width):
    return jnp.pad(w, ((0, 0), (0, width - w.shape[1])))


def _swap_halves(w):
    half = w.shape[-1] // 2
    return jnp.concatenate([w[..., half:], w[..., :half]], axis=-1)


def _rope_gains(g):
    rope = g[MLA_NOPE:]
    return jnp.stack([g[:MLA_NOPE],
                      jnp.pad(rope, (0, LANES - MLA_ROPE)),
                      jnp.pad(_swap_halves(rope), (0, LANES - MLA_ROPE))])


def kernel(x, positions, lb_logits, l0_norm_mix, l0_w_in, l0_hgrn_out_norm, l0_gmlp_v_norm, l0_gmlp_w_s, l0_gmlp_b_s, l0_w_out, l0_norm_ffn, l0_peer_w_query, l0_peer_sub_keys, l0_peer_expert_down, l0_peer_expert_up, l1_norm_mix, l1_w_in, l1_mla_cq_norm, l1_mla_ckv_norm, l1_mla_w_uq, l1_mla_w_ukv, l1_mla_q_norm, l1_mla_k_norm, l1_moba_q_norm, l1_moba_k_norm, l1_w_out, l1_norm_ffn, l1_peer_w_query, l1_peer_sub_keys, l1_peer_expert_down, l1_peer_expert_up):
    batch, seq, d = x.shape
    n = batch * seq
    x2d = x.reshape(n, d)

    lb0 = jax.nn.softmax(lb_logits.astype(F32), axis=0)[0]
    z0 = _norm_matmul(x2d, l0_norm_mix, l0_w_in.astype(BF16))
    out_a, out_b = _l0_mixer(z0, lb0, l0_hgrn_out_norm, l0_gmlp_v_norm, l0_gmlp_w_s, l0_gmlp_b_s, batch, seq)
    half = out_a.shape[1]
    w_out0 = l0_w_out.astype(BF16)
    x2d = _out_proj(x2d, out_a, out_b, w_out0[:half], w_out0[half:])
    x2d = _peer_ffn(x2d, l0_norm_ffn, l0_peer_w_query, l0_peer_sub_keys, l0_peer_expert_down, l0_peer_expert_up)

    q_lora, kv_lora = l1_mla_cq_norm.shape[0], l1_mla_ckv_norm.shape[0]
    c0 = q_lora + kv_lora
    dw = MOBA_HEADS * LANES
    w1 = l1_w_in
    k_pe_w = w1[:, c0:c0 + MLA_ROPE]
    moba_w = w1[:, c0 + MLA_ROPE:]
    w1p = jnp.concatenate([moba_w, w1[:, :c0], _pad_cols(k_pe_w, LANES),
                           _pad_cols(_swap_halves(k_pe_w), LANES)], axis=1).astype(BF16)
    z1 = _norm_matmul(x2d, l1_norm_mix, w1p)

    half_r = MLA_ROPE // 2
    inv_freq = ROPE_THETA ** (-jnp.arange(half_r, dtype=F32) / half_r)
    ang = positions.reshape(n, 1).astype(F32) * inv_freq
    zeros = jnp.zeros((n, LANES - MLA_ROPE), F32)
    cos_t = jnp.concatenate([jnp.cos(ang), jnp.cos(ang), zeros], axis=1)
    sin_t = jnp.concatenate([-jnp.sin(ang), jnp.sin(ang), zeros], axis=1)

    wuq = l1_mla_w_uq.reshape(q_lora, MLA_HEADS, MLA_QK)
    wuq_p = jnp.concatenate(
        [wuq[..., :MLA_NOPE],
         jnp.pad(wuq[..., MLA_NOPE:], ((0, 0), (0, 0), (0, LANES - MLA_ROPE))),
         jnp.pad(_swap_halves(wuq[..., MLA_NOPE:]), ((0, 0), (0, 0), (0, LANES - MLA_ROPE)))],
        axis=-1).reshape(q_lora, MLA_HEADS * 3 * LANES).astype(BF16)
    q_c, k_c, v_c = _mla_prep(z1, cos_t, sin_t, l1_mla_cq_norm.reshape(1, -1), l1_mla_ckv_norm.reshape(1, -1),
                              wuq_p, l1_mla_w_ukv.astype(BF16), _rope_gains(l1_mla_q_norm),
                              _rope_gains(l1_mla_k_norm), col0=3 * MOBA_HEADS)
    out_c = _mla_attention(q_c, k_c, v_c, batch, seq)

    nb = seq // MOBA_BLOCK
    qn, kn, vb, kmean = _moba_norm(z1, l1_moba_q_norm, l1_moba_k_norm,
                                   col_q=0, col_k=MOBA_HEADS, col_v=2 * MOBA_HEADS, nb=nb)
    out_d = _moba_attention(qn, kn, vb, kmean.reshape(batch, nb, dw), batch, nb)

    w_out1 = l1_w_out.astype(BF16)
    x2d = _out_proj(x2d, out_c, out_d, w_out1[:out_c.shape[1]], w_out1[out_c.shape[1]:])
    x2d = _peer_ffn(x2d, l1_norm_ffn, l1_peer_w_query, l1_peer_sub_keys, l1_peer_expert_down, l1_peer_expert_up)
    return x2d.reshape(batch, seq, d)
```

```python
import functools

import numpy as np
import jax
import jax.numpy as jnp
from jax import lax
from jax.experimental import pallas as pl
from jax.experimental.pallas import tpu as pltpu

F32 = jnp.float32
BF16 = jnp.bfloat16

RMS_EPS = 1e-6
LANES = 128
VMEM_LIMIT = 56 * 1024 * 1024

HGRN_HEADS = 4
GMLP_GROUPS = 4
MIX_CHUNK = 128
MLA_HEADS = 4
MLA_NOPE = 128
MLA_ROPE = 64
MLA_QK = MLA_NOPE + MLA_ROPE
ROPE_THETA = 10000.0
MOBA_HEADS = 4
MOBA_BLOCK = 256
MOBA_TOPK = 3
PEER_HEADS = 8
PEER_KEYS = 128
PEER_TOPK = 16
NEG_BIG = -1e30


def _cparams(semantics):
    return pltpu.CompilerParams(dimension_semantics=semantics, vmem_limit_bytes=VMEM_LIMIT)


def _dot(a, b):
    return jnp.dot(a, b, preferred_element_type=F32)


def _dot_nt(a, b):
    return lax.dot_general(a, b, (((1,), (1,)), ((), ())), preferred_element_type=F32)


def _dot_tn(a, b):
    return lax.dot_general(a, b, (((0,), (0,)), ((), ())), preferred_element_type=F32)


def _rms(x, gain):
    return x * lax.rsqrt(jnp.mean(x * x, axis=-1, keepdims=True) + RMS_EPS) * gain


def _gelu(x):
    c = np.float32(np.sqrt(2.0 / np.pi))
    half = 0.5 * x
    return half + half * jnp.tanh(x * (c + np.float32(c * 0.044715) * (x * x)))


def _silu(x):
    return x * jax.nn.sigmoid(x)


def _norm_matmul_body(x_ref, g_ref, w_ref, o_ref):
    y = _rms(x_ref[...], g_ref[...])
    o_ref[...] = _dot(y.astype(BF16), w_ref[...])


def _norm_matmul(x2d, gain, w_bf16, tm=512):
    n, d = x2d.shape
    m = w_bf16.shape[1]
    return pl.pallas_call(
        _norm_matmul_body,
        grid=(n // tm,),
        in_specs=[pl.BlockSpec((tm, d), lambda i: (i, 0)),
                  pl.BlockSpec((1, d), lambda i: (0, 0)),
                  pl.BlockSpec((d, m), lambda i: (0, 0))],
        out_specs=pl.BlockSpec((tm, m), lambda i: (i, 0)),
        out_shape=jax.ShapeDtypeStruct((n, m), F32),
        compiler_params=_cparams(("parallel",)),
    )(x2d, gain.reshape(1, d), w_bf16)


_HGRN_LEVELS = (64, 32, 16, 8, 4, 2, 1)


def _hgrn_exponent_matrix(c):
    a = np.zeros((len(_HGRN_LEVELS) + 1, c, c), np.float32)
    r = np.arange(c)[:, None]
    u = np.arange(c)[None, :]
    a[0] = (u <= r)
    for li, w in enumerate(_HGRN_LEVELS):
        p = (r // (2 * w)) * (2 * w) + w
        second = r >= p
        a[li + 1] = np.where(second, (u >= p) & (u <= r), (u > r) & (u <= p - 1))
    return a.reshape(-1, c)


_L0_HEADS_PER_STEP = 2


def _hgrn_head(q, fg, inp, og, lb, out_gain, amat, state_t):
    c = MIX_CHUNK
    row = lax.broadcasted_iota(jnp.int32, (c, c), 0)
    col = lax.broadcasted_iota(jnp.int32, (c, c), 1)
    rowc = lax.broadcasted_iota(jnp.int32, (c, 1), 0)
    f = lb + (1.0 - lb) * jax.nn.sigmoid(fg)
    k = 1.0 - f
    g = jnp.log(f)
    val = _silu(inp)
    g1 = g.astype(BF16)
    r1 = g - g1.astype(F32)
    g2 = r1.astype(BF16)
    g3 = (r1 - g2.astype(F32)).astype(BF16)
    dall = _dot(amat, g1) + _dot(amat, g2) + _dot(amat, g3)
    b = dall[0:c]

    scores = jnp.where(row == col, _dot_nt(q.astype(BF16), k.astype(BF16)), 0.0)
    for li, w in enumerate(_HGRN_LEVELS):
        e = jnp.exp(dall[(li + 1) * c:(li + 2) * c])
        sh = int(np.log2(w))
        second = (jnp.right_shift(rowc, sh) & 1) == 1
        qt = jnp.where(second, q * e, 0.0).astype(BF16)
        kt = jnp.where(second, 0.0, k * e).astype(BF16)
        same = jnp.right_shift(row, sh + 1) == jnp.right_shift(col, sh + 1)
        scores = scores + jnp.where(same, _dot_nt(qt, kt), 0.0)
    val16 = val.astype(BF16)
    o = _dot(scores.astype(BF16), val16)
    o = o + _dot_nt((q * jnp.exp(b)).astype(BF16), state_t.astype(BF16))
    b_last = b[c - 1:c]
    k_dec = (k * jnp.exp(b_last - b)).astype(BF16)
    new_state = state_t * jnp.exp(b_last) + _dot_tn(val16, k_dec)
    return _rms(o, out_gain) * _silu(og), new_state


def _gmlp_group(u, v, v_gain, w_s, b_s):
    c = MIX_CHUNK
    row = lax.broadcasted_iota(jnp.int32, (c, c), 0)
    col = lax.broadcasted_iota(jnp.int32, (c, c), 1)
    vn = _rms(_gelu(v), v_gain)
    w_causal = jnp.where(row >= col, w_s, 0.0).astype(BF16)
    return _gelu(u) * (_dot(w_causal, vn.astype(BF16)) + b_s)


def _l0_mix_body(q_ref, fg_ref, inp_ref, og_ref, u_ref, v_ref, lb_ref, on_ref, vn_ref,
                 ws_ref, bs_ref, amat_ref, oa_ref, ob_ref, state_ref):
    @pl.when(pl.program_id(2) == 0)
    def _():
        state_ref[...] = jnp.zeros_like(state_ref)

    amat = amat_ref[...]
    for p in range(_L0_HEADS_PER_STEP):
        lanes = slice(p * LANES, (p + 1) * LANES)
        oa, new_state = _hgrn_head(q_ref[:, lanes], fg_ref[:, lanes], inp_ref[:, lanes], og_ref[:, lanes],
                                   lb_ref[:, lanes], on_ref[:, lanes], amat, state_ref[p])
        state_ref[p] = new_state
        oa_ref[:, lanes] = oa
        ob_ref[:, lanes] = _gmlp_group(u_ref[:, lanes], v_ref[:, lanes], vn_ref[:, lanes], ws_ref[p], bs_ref[p])


def _l0_mixer(z, lb, out_norm, v_norm, w_s, b_s, batch, seq):
    n = z.shape[0]
    c = MIX_CHUNK
    nc = seq // c
    hp = _L0_HEADS_PER_STEP
    width = hp * LANES
    steps = HGRN_HEADS // hp
    amat = jnp.asarray(_hgrn_exponent_matrix(c), BF16)
    na = amat.shape[0]

    def zcol(group):
        return pl.BlockSpec((c, width), lambda b, hh, i, g=group: (b * nc + i, g * steps + hh))

    def vec():
        return pl.BlockSpec((1, width), lambda b, hh, i: (0, hh))

    out_spec = pl.BlockSpec((c, width), lambda b, hh, i: (b * nc + i, hh))
    return pl.pallas_call(
        _l0_mix_body,
        grid=(batch, steps, nc),
        in_specs=[zcol(0), zcol(1), zcol(2), zcol(3), zcol(4), zcol(5),
                  vec(), vec(), vec(),
                  pl.BlockSpec((hp, c, c), lambda b, hh, i: (hh, 0, 0)),
                  pl.BlockSpec((hp, c, 1), lambda b, hh, i: (hh, 0, 0)),
                  pl.BlockSpec((na, c), lambda b, hh, i: (0, 0))],
        out_specs=[out_spec, out_spec],
        out_shape=[jax.ShapeDtypeStruct((n, HGRN_HEADS * LANES), F32),
                   jax.ShapeDtypeStruct((n, HGRN_HEADS * LANES), F32)],
        scratch_shapes=[pltpu.VMEM((hp, LANES, LANES), F32)],
        compiler_params=_cparams(("parallel", "parallel", "arbitrary")),
    )(z, z, z, z, z, z, lb.reshape(1, -1), out_norm.reshape(1, -1), v_norm.reshape(1, -1),
      w_s, b_s.reshape(GMLP_GROUPS, c, 1), amat)


def _mla_prep_body(cq_ref, ckv_ref, kpe_ref, kpes_ref, cos_ref, sin_ref, cqn_ref, ckvn_ref,
                   wuq_ref, wukv_ref, gq_ref, gk_ref, q_out, k_out, v_out):
    cos = cos_ref[...]
    sin = sin_ref[...]
    qn = _rms(cq_ref[...], cqn_ref[...])
    kvn = _rms(ckv_ref[...], ckvn_ref[...])
    qall = _dot(qn.astype(BF16), wuq_ref[...])
    kvall = _dot(kvn.astype(BF16), wukv_ref[...])
    kpe = kpe_ref[...]
    kpes = kpes_ref[...]
    kpe_sq = jnp.sum(kpe * kpe, axis=-1, keepdims=True)
    gq = gq_ref[...]
    gk = gk_ref[...]
    scale = np.float32(MLA_QK ** -0.5)
    for h in range(MLA_HEADS):
        nope = qall[:, 3 * h * LANES:(3 * h + 1) * LANES]
        rp = qall[:, (3 * h + 1) * LANES:(3 * h + 2) * LANES]
        rps = qall[:, (3 * h + 2) * LANES:(3 * h + 3) * LANES]
        ms = (jnp.sum(nope * nope, axis=-1, keepdims=True)
              + jnp.sum(rp * rp, axis=-1, keepdims=True)) * np.float32(1.0 / MLA_QK)
        r = lax.rsqrt(ms + RMS_EPS) * scale
        q_out[:, 2 * h * LANES:(2 * h + 1) * LANES] = (nope * r * gq[0:1]).astype(BF16)
        q_out[:, (2 * h + 1) * LANES:(2 * h + 2) * LANES] = (
            (rp * r * gq[1:2]) * cos + (rps * r * gq[2:3]) * sin).astype(BF16)
        knope = kvall[:, 2 * h * LANES:(2 * h + 1) * LANES]
        ms = (jnp.sum(knope * knope, axis=-1, keepdims=True) + kpe_sq) * np.float32(1.0 / MLA_QK)
        r = lax.rsqrt(ms + RMS_EPS)
        k_out[:, 2 * h * LANES:(2 * h + 1) * LANES] = (knope * r * gk[0:1]).astype(BF16)
        k_out[:, (2 * h + 1) * LANES:(2 * h + 2) * LANES] = (
            (kpe * r * gk[1:2]) * cos + (kpes * r * gk[2:3]) * sin).astype(BF16)
        v_out[:, h * LANES:(h + 1) * LANES] = kvall[:, (2 * h + 1) * LANES:(2 * h + 2) * LANES].astype(BF16)


def _mla_prep(z, cos, sin, cq_norm, ckv_norm, wuq_p, wukv, gq, gk, col0, tm=256):
    n = z.shape[0]
    hq = MLA_HEADS * 2 * LANES
    full = lambda a: pl.BlockSpec(a.shape, lambda i: (0,) * a.ndim)
    return pl.pallas_call(
        _mla_prep_body,
        grid=(n // tm,),
        in_specs=[pl.BlockSpec((tm, 2 * LANES), lambda i: (i, col0 // 2)),
                  pl.BlockSpec((tm, LANES), lambda i: (i, col0 + 2)),
                  pl.BlockSpec((tm, LANES), lambda i: (i, col0 + 3)),
                  pl.BlockSpec((tm, LANES), lambda i: (i, col0 + 4)),
                  pl.BlockSpec((tm, LANES), lambda i: (i, 0)),
                  pl.BlockSpec((tm, LANES), lambda i: (i, 0)),
                  full(cq_norm), full(ckv_norm), full(wuq_p), full(wukv), full(gq), full(gk)],
        out_specs=[pl.BlockSpec((tm, hq), lambda i: (i, 0)),
                   pl.BlockSpec((tm, hq), lambda i: (i, 0)),
                   pl.BlockSpec((tm, MLA_HEADS * LANES), lambda i: (i, 0))],
        out_shape=[jax.ShapeDtypeStruct((n, hq), BF16),
                   jax.ShapeDtypeStruct((n, hq), BF16),
                   jax.ShapeDtypeStruct((n, MLA_HEADS * LANES), BF16)],
        compiler_params=_cparams(("parallel",)),
    )(z, z, z, z, cos, sin, cq_norm, ckv_norm, wuq_p, wukv, gq, gk)


def _flash_scores(q, keys_masks):
    return [jnp.where(mask, _dot_nt(q, k), NEG_BIG) for k, mask in keys_masks]


def _flash_apply(state, scores, values):
    m_prev, l_prev, acc = state
    m_new = m_prev
    for s in scores:
        m_new = jnp.maximum(m_new, jnp.max(s, axis=-1, keepdims=True))
    alpha = jnp.exp(m_prev - m_new)
    l_new = alpha * l_prev
    acc = alpha * acc
    for s, v in zip(scores, values):
        p = jnp.exp(s - m_new)
        l_new = l_new + jnp.sum(p, axis=-1, keepdims=True)
        acc = acc + _dot(p.astype(BF16), v)
    return m_new, l_new, acc


def _flash_loop(steps, tq, dv, scores_of, values_of):
    state = (jnp.full((tq, 1), NEG_BIG, F32), jnp.zeros((tq, 1), F32), jnp.zeros((tq, dv), F32))
    state = lax.fori_loop(0, steps, lambda i, st: _flash_apply(st, scores_of(i), values_of(i)), state)
    return state[2] / state[1]


def _mla_attn_body(q_ref, k_ref, v_ref, o_ref, *, tk):
    qi = pl.program_id(2)
    q = q_ref[...]
    tq = q.shape[0]
    row = lax.broadcasted_iota(jnp.int32, (tq, tk), 0)
    col = lax.broadcasted_iota(jnp.int32, (tq, tk), 1)
    ahead = col - row

    def scores_of(n):
        start = pl.multiple_of(n * tk, tk)
        mask = ahead <= qi * tq - n * tk
        return _flash_scores(q, [(k_ref[pl.ds(start, tk), :], mask)])

    def values_of(n):
        return [v_ref[pl.ds(pl.multiple_of(n * tk, tk), tk), :]]

    o_ref[...] = _flash_loop((qi + 1) * (tq // tk), tq, v_ref.shape[1], scores_of, values_of)


def _mla_attention(q, k, v, batch, seq, tq=512, tk=512):
    n = q.shape[0]
    tq = min(tq, seq)
    nt = seq // tq
    return pl.pallas_call(
        functools.partial(_mla_attn_body, tk=tk),
        grid=(batch, MLA_HEADS, nt),
        in_specs=[pl.BlockSpec((tq, 2 * LANES), lambda b, h, qi: (b * nt + qi, h)),
                  pl.BlockSpec((seq, 2 * LANES), lambda b, h, qi: (b, h)),
                  pl.BlockSpec((seq, LANES), lambda b, h, qi: (b, h))],
        out_specs=pl.BlockSpec((tq, LANES), lambda b, h, qi: (b * nt + qi, h)),
        out_shape=jax.ShapeDtypeStruct((n, MLA_HEADS * LANES), F32),
        compiler_params=_cparams(("parallel", "parallel", "arbitrary")),
    )(q, k, v)


def _moba_norm_body(q_ref, k_ref, v_ref, gq_ref, gk_ref, qn_ref, kn_ref, vb_ref, km_ref, *, nb):
    block_id = pl.program_id(0) % nb
    lane = lax.broadcasted_iota(jnp.int32, (q_ref.shape[0], LANES), 1)
    one_hot = jnp.where(lane == block_id, 1.0, 0.0).astype(BF16)
    vb_ref[...] = v_ref[...].astype(BF16)
    for h in range(MOBA_HEADS):
        lanes = slice(h * LANES, (h + 1) * LANES)
        qn_ref[:, lanes] = (_rms(q_ref[:, lanes], gq_ref[...]) * np.float32(LANES ** -0.5)).astype(BF16)
        kn = _rms(k_ref[:, lanes], gk_ref[...])
        kn_ref[:, 2 * h * LANES:(2 * h + 1) * LANES] = kn.astype(BF16)
        kn_ref[:, (2 * h + 1) * LANES:(2 * h + 2) * LANES] = one_hot
        km_ref[0, :, lanes] = jnp.mean(kn, axis=0, keepdims=True)


def _moba_norm(z, gq, gk, col_q, col_k, col_v, nb):
    n = z.shape[0]
    h = MOBA_HEADS
    width = h * LANES
    nblk = n // MOBA_BLOCK
    blk = lambda c0: pl.BlockSpec((MOBA_BLOCK, width), lambda i: (i, c0 // h))
    vec = pl.BlockSpec((1, LANES), lambda i: (0, 0))
    return pl.pallas_call(
        functools.partial(_moba_norm_body, nb=nb),
        grid=(nblk,),
        in_specs=[blk(col_q), blk(col_k), blk(col_v), vec, vec],
        out_specs=[blk(0), pl.BlockSpec((MOBA_BLOCK, 2 * width), lambda i: (i, 0)), blk(0),
                   pl.BlockSpec((1, 1, width), lambda i: (i, 0, 0))],
        out_shape=[jax.ShapeDtypeStruct((n, width), BF16), jax.ShapeDtypeStruct((n, 2 * width), BF16),
                   jax.ShapeDtypeStruct((n, width), BF16),
                   jax.ShapeDtypeStruct((nblk, 1, width), F32)],
        compiler_params=_cparams(("parallel",)),
    )(z, z, z, gq.reshape(1, -1), gk.reshape(1, -1))


_MOBA_GROUP = 8


def _moba_attn_body(q_ref, k_ref, v_ref, km_ref, o_ref, *, nb):
    j = pl.program_id(2)
    q = q_ref[...]
    t = q.shape[0]

    km = km_ref[0]
    if nb < LANES:
        km = jnp.concatenate([km, jnp.zeros((LANES - nb, km.shape[1]), F32)], axis=0)
    gate = _dot_nt(q, km.astype(BF16))
    lane = lax.broadcasted_iota(jnp.int32, gate.shape, 1)
    past = lane < j
    masked = jnp.where(past, gate, -jnp.inf)
    s = masked
    kth = None
    for _ in range(MOBA_TOPK):
        kth = jnp.max(s, axis=-1, keepdims=True)
        s = jnp.where(s == kth, -jnp.inf, s)
    attend = (past & (masked >= kth)) | (lane == j)
    q_ext = jnp.concatenate([q, jnp.where(attend, 0.0, NEG_BIG).astype(BF16)], axis=1)

    row = lax.broadcasted_iota(jnp.int32, (t, t), 0)
    col = lax.broadcasted_iota(jnp.int32, (t, t), 1)
    ahead = col - row
    groups = (j + _MOBA_GROUP) // _MOBA_GROUP

    def blocks_of(it):
        return [(groups - 1 - it) * _MOBA_GROUP + c for c in range(_MOBA_GROUP)]

    def scores_of(it):
        return _flash_scores(q_ext, [(k_ref[pl.ds(pl.multiple_of(n * t, t), t), :],
                                      ahead <= jnp.where(n == j, 0, t))
                                     for n in blocks_of(it)])

    def values_of(it):
        return [v_ref[pl.ds(pl.multiple_of(n * t, t), t), :] for n in blocks_of(it)]

    o_ref[...] = _flash_loop(groups, t, v_ref.shape[1], scores_of, values_of)


def _moba_attention(qn, kn, vb, kmean, batch, nb):
    n = qn.shape[0]
    h = MOBA_HEADS
    blk = MOBA_BLOCK
    seq = nb * blk
    assert nb % _MOBA_GROUP == 0, "the grouped block loop reads whole groups of resident key blocks"
    return pl.pallas_call(
        functools.partial(_moba_attn_body, nb=nb),
        grid=(batch, h, nb),
        in_specs=[pl.BlockSpec((blk, LANES), lambda b, hh, j: (b * nb + j, hh)),
                  pl.BlockSpec((seq, 2 * LANES), lambda b, hh, j: (b, hh)),
                  pl.BlockSpec((seq, LANES), lambda b, hh, j: (b, hh)),
                  pl.BlockSpec((1, nb, LANES), lambda b, hh, j: (b, 0, hh))],
        out_specs=pl.BlockSpec((blk, LANES), lambda b, hh, j: (b * nb + j, hh)),
        out_shape=jax.ShapeDtypeStruct((n, h * LANES), F32),
        compiler_params=_cparams(("parallel", "parallel", "arbitrary")),
    )(qn, kn, vb, kmean)


_SUBLANES = 8


def _all_sublanes(x, op):
    for shift in (4, 2, 1):
        x = op(x, pltpu.roll(x, shift, axis=0))
    return x


def _bitonic_merge(v):
    v = list(v)
    for j in (8, 4, 2, 1):
        for i in range(16):
            if i & j == 0:
                v[i], v[i | j] = jnp.maximum(v[i], v[i | j]), jnp.minimum(v[i], v[i | j])
    return v


def _top16(v):
    v = list(v)
    k = 2
    while k <= 16:
        j = k // 2
        while j >= 1:
            for i in range(16):
                l = i ^ j
                if l > i:
                    hi, lo = jnp.maximum(v[i], v[l]), jnp.minimum(v[i], v[l])
                    v[i], v[l] = (hi, lo) if i & k == 0 else (lo, hi)
            j //= 2
        k *= 2
    for shift in (4, 2, 1):
        v = _bitonic_merge([jnp.maximum(v[i], pltpu.roll(v[15 - i], shift, axis=0)) for i in range(16)])
    return v


def _next_largest(tiles, kth):
    count = jnp.zeros_like(kth)
    below = jnp.full_like(kth, -jnp.inf)
    for x in tiles:
        count = count + jnp.where(x >= kth, 1.0, 0.0)
        below = jnp.maximum(below, jnp.where(x < kth, x, -jnp.inf))
    count = _all_sublanes(count, jnp.add)
    below = _all_sublanes(below, jnp.maximum)
    return jnp.where(count > PEER_TOPK + 0.5, kth, below)


def _on_sublanes(tiles):
    sub = lax.broadcasted_iota(jnp.int32, tiles[0].shape, 0)
    out = tiles[0]
    for b in range(1, _SUBLANES):
        out = jnp.where(sub == b, tiles[b], out)
    return out


def _peer_route_body(x_ref, a_ref, b_ref, wa_ref, wb_ref, g_ref, wq_ref, sk_ref,
                     xo_ref, xnt_ref, thr_ref, s2_ref, e1_ref, e2_ref, q_scr):
    h = pl.program_id(1)

    @pl.when(h == 0)
    def _():
        x = (x_ref[...] + _dot(a_ref[...].astype(BF16), wa_ref[...])
             + _dot(b_ref[...].astype(BF16), wb_ref[...]))
        xo_ref[...] = x
        xn = _rms(x, g_ref[...])
        xnt_ref[...] = xn.T.astype(BF16)
        q = _dot(xn.astype(BF16), wq_ref[...])
        for hp in range(2 * PEER_HEADS):
            q_scr[hp] = q[:, hp * LANES:(hp + 1) * LANES].astype(BF16)

    s1_all = _dot_nt(sk_ref[0, 0], q_scr[2 * h])
    s2_all = _dot_nt(sk_ref[0, 1], q_scr[2 * h + 1])
    for lt in range(s1_all.shape[1] // LANES):
        lanes = slice(lt * LANES, (lt + 1) * LANES)
        thr, e1, e2 = _peer_route_tile(s1_all[:, lanes], s2_all[:, lanes])
        thr_ref[0, lt] = thr
        s2_ref[0, lt] = s2_all[:, lanes]
        e1_ref[0, lt] = e1
        e2_ref[0, lt] = e2


def _peer_route_tile(s1, s2):
    tiles1 = [s1[v * _SUBLANES:(v + 1) * _SUBLANES] for v in range(PEER_KEYS // _SUBLANES)]
    tiles2 = [s2[v * _SUBLANES:(v + 1) * _SUBLANES] for v in range(PEER_KEYS // _SUBLANES)]
    top1 = _top16(tiles1)
    top2 = _top16(tiles2)
    next1 = _next_largest(tiles1, top1[-1])
    next2 = _next_largest(tiles2, top2[-1])
    col1 = [_on_sublanes(top1[:_SUBLANES]), _on_sublanes(top1[_SUBLANES:])]
    col2 = [_on_sublanes(top2[:_SUBLANES]), _on_sublanes(top2[_SUBLANES:])]
    cand = [top1[a] + col2[0] for a in range(_SUBLANES)] + [top1[0] + col2[1], top2[0] + col1[1]]
    pad = jnp.full_like(s1[:_SUBLANES], -jnp.inf)
    kth = _top16(cand + [pad] * (16 - len(cand)))[-1]
    nxt = jnp.maximum(_next_largest(cand, kth), jnp.maximum(next1 + top2[0], top1[0] + next2))
    tau = 0.5 * (kth + nxt)
    m1 = top1[0]
    m2 = top2[0]
    z = jnp.zeros_like(tau)
    for c in cand:
        z = z + jnp.where(c >= tau, jnp.exp(c - (m1 + m2)), 0.0)
    inv_z = 1.0 / _all_sublanes(z, jnp.add)
    thr = jnp.concatenate([tau - x for x in tiles1], axis=0)
    e1 = jnp.concatenate([jnp.exp(x - m1) * inv_z for x in tiles1], axis=0)
    e2 = jnp.concatenate([jnp.exp(x - m2) for x in tiles2], axis=0)
    return thr, e1, e2


def _peer_route(x2d, a, b, wa_bf16, wb_bf16, gain, wq_bf16, sk_bf16, t=512):
    n, d = x2d.shape
    ka, kb = a.shape[1], b.shape[1]
    hq = wq_bf16.shape[1]
    routed = jax.ShapeDtypeStruct((PEER_HEADS, n // LANES, PEER_KEYS, LANES), F32)
    rspec = pl.BlockSpec((1, t // LANES, PEER_KEYS, LANES), lambda i, h: (h, i, 0, 0))
    rows = lambda width: pl.BlockSpec((t, width), lambda i, h: (i, 0))
    whole = lambda r, c: pl.BlockSpec((r, c), lambda i, h: (0, 0))
    return pl.pallas_call(
        _peer_route_body,
        grid=(n // t, PEER_HEADS),
        in_specs=[rows(d), rows(ka), rows(kb), whole(ka, d), whole(kb, d), whole(1, d), whole(d, hq),
                  pl.BlockSpec((1, 2, PEER_KEYS, LANES), lambda i, h: (h, 0, 0, 0))],
        out_specs=[rows(d), pl.BlockSpec((d, t), lambda i, h: (0, i)), rspec, rspec, rspec, rspec],
        out_shape=[jax.ShapeDtypeStruct((n, d), F32), jax.ShapeDtypeStruct((d, n), BF16),
                   routed, routed, routed, routed],
        scratch_shapes=[pltpu.VMEM((hq // LANES, t, LANES), BF16)],
        compiler_params=_cparams(("parallel", "arbitrary")),
    )(x2d, a, b, wa_bf16, wb_bf16, gain.reshape(1, d), wq_bf16, sk_bf16)


_PEER_ROWS = 2


def _peer_main_body(x_ref, xnt_ref, thr_ref, s2_ref, e1_ref, e2_ref, down_ref, upt_ref, o_ref,
                    acc_ref, act_ref, p_ref, *, bi):
    e = pl.program_id(1)
    t = act_ref.shape[1]
    nj = PEER_KEYS // _SUBLANES
    row0 = pl.multiple_of(e * bi, bi)

    @pl.when(e == 0)
    def _():
        acc_ref[...] = jnp.zeros_like(acc_ref)

    act_ref[...] = _gelu(_dot(down_ref[...], xnt_ref[...]))

    for lt in range(t // LANES):
        lanes = slice(lt * LANES, (lt + 1) * LANES)

        def row_group(g, carry, lt=lt, lanes=lanes):
            i0 = g * _PEER_ROWS
            w = [[jnp.zeros((_SUBLANES, LANES), F32) for _ in range(nj)] for _ in range(_PEER_ROWS)]
            for h in range(PEER_HEADS):
                thr = [jnp.broadcast_to(thr_ref[h, lt, pl.ds(row0 + i0 + r, 1), :], (_SUBLANES, LANES))
                       for r in range(_PEER_ROWS)]
                e1 = [jnp.broadcast_to(e1_ref[h, lt, pl.ds(row0 + i0 + r, 1), :], (_SUBLANES, LANES))
                      for r in range(_PEER_ROWS)]
                for jv in range(nj):
                    rows = slice(jv * _SUBLANES, (jv + 1) * _SUBLANES)
                    s2 = s2_ref[h, lt, rows, :]
                    e2 = e2_ref[h, lt, rows, :]
                    for r in range(_PEER_ROWS):
                        w[r][jv] = w[r][jv] + jnp.where(s2 >= thr[r], e2, 0.0) * e1[r]
            for r in range(_PEER_ROWS):
                for jp in range(nj // 2):
                    start = pl.multiple_of((i0 + r) * PEER_KEYS + jp * 2 * _SUBLANES, 2 * _SUBLANES)
                    rows = pl.ds(start, 2 * _SUBLANES)
                    wt = jnp.concatenate([w[r][2 * jp], w[r][2 * jp + 1]], axis=0)
                    p_ref[rows, lanes] = (wt * act_ref[rows, lanes]).astype(BF16)
            return carry

        lax.fori_loop(0, bi // _PEER_ROWS, row_group, 0)

    acc_ref[...] += _dot(upt_ref[...], p_ref[...])

    @pl.when(e == pl.num_programs(1) - 1)
    def _():
        o_ref[...] = x_ref[...] + acc_ref[...].T


def _peer_main(x2d, xnt, thr, s2, e1, e2, down_bf16, upt_bf16, t=512, bi=16):
    n, d = x2d.shape
    ne = down_bf16.shape[0]
    be = bi * PEER_KEYS
    rspec = pl.BlockSpec((PEER_HEADS, t // LANES, PEER_KEYS, LANES), lambda i, e: (0, i, 0, 0))
    return pl.pallas_call(
        functools.partial(_peer_main_body, bi=bi),
        grid=(n // t, ne // be),
        in_specs=[pl.BlockSpec((t, d), lambda i, e: (i, 0)),
                  pl.BlockSpec((d, t), lambda i, e: (0, i)),
                  rspec, rspec, rspec, rspec,
                  pl.BlockSpec((be, d), lambda i, e: (e, 0)),
                  pl.BlockSpec((d, be), lambda i, e: (0, e))],
        out_specs=pl.BlockSpec((t, d), lambda i, e: (i, 0)),
        out_shape=jax.ShapeDtypeStruct((n, d), F32),
        scratch_shapes=[pltpu.VMEM((d, t), F32), pltpu.VMEM((be, t), F32), pltpu.VMEM((be, t), BF16)],
        compiler_params=_cparams(("parallel", "arbitrary")),
    )(x2d, xnt, thr, s2, e1, e2, down_bf16, upt_bf16)


def _mixer_out_and_peer_ffn(x2d, mix_a, mix_b, w_out, gain, w_query, sub_keys, expert_down, expert_up):
    w_out = w_out.astype(BF16)
    ka = mix_a.shape[1]
    x1, xnt, thr, s2, e1, e2 = _peer_route(x2d, mix_a, mix_b, w_out[:ka], w_out[ka:], gain,
                                           w_query.astype(BF16), sub_keys.astype(BF16))
    return _peer_main(x1, xnt, thr, s2, e1, e2, expert_down.astype(BF16), expert_up.T.astype(BF16))


def _pad_cols(w, width):
    return jnp.pad(w, ((0, 0), (0, width - w.shape[1])))


def _swap_halves(w):
    half = w.shape[-1] // 2
    return jnp.concatenate([w[..., half:], w[..., :half]], axis=-1)


def _rope_gains(g):
    rope = g[MLA_NOPE:]
    return jnp.stack([g[:MLA_NOPE],
                      jnp.pad(rope, (0, LANES - MLA_ROPE)),
                      jnp.pad(_swap_halves(rope), (0, LANES - MLA_ROPE))])


def kernel(x, positions, lb_logits, l0_norm_mix, l0_w_in, l0_hgrn_out_norm, l0_gmlp_v_norm, l0_gmlp_w_s, l0_gmlp_b_s, l0_w_out, l0_norm_ffn, l0_peer_w_query, l0_peer_sub_keys, l0_peer_expert_down, l0_peer_expert_up, l1_norm_mix, l1_w_in, l1_mla_cq_norm, l1_mla_ckv_norm, l1_mla_w_uq, l1_mla_w_ukv, l1_mla_q_norm, l1_mla_k_norm, l1_moba_q_norm, l1_moba_k_norm, l1_w_out, l1_norm_ffn, l1_peer_w_query, l1_peer_sub_keys, l1_peer_expert_down, l1_peer_expert_up):
    batch, seq, d = x.shape
    n = batch * seq
    x2d = x.reshape(n, d)

    lb0 = jax.nn.softmax(lb_logits.astype(F32), axis=0)[0]
    z0 = _norm_matmul(x2d, l0_norm_mix, l0_w_in.astype(BF16))
    out_a, out_b = _l0_mixer(z0, lb0, l0_hgrn_out_norm, l0_gmlp_v_norm, l0_gmlp_w_s, l0_gmlp_b_s, batch, seq)
    x2d = _mixer_out_and_peer_ffn(x2d, out_a, out_b, l0_w_out, l0_norm_ffn, l0_peer_w_query, l0_peer_sub_keys,
                                  l0_peer_expert_down, l0_peer_expert_up)

    q_lora, kv_lora = l1_mla_cq_norm.shape[0], l1_mla_ckv_norm.shape[0]
    c0 = q_lora + kv_lora
    dw = MOBA_HEADS * LANES
    w1 = l1_w_in
    k_pe_w = w1[:, c0:c0 + MLA_ROPE]
    moba_w = w1[:, c0 + MLA_ROPE:]
    w1p = jnp.concatenate([moba_w, w1[:, :c0], _pad_cols(k_pe_w, LANES),
                           _pad_cols(_swap_halves(k_pe_w), LANES)], axis=1).astype(BF16)
    z1 = _norm_matmul(x2d, l1_norm_mix, w1p)

    half_r = MLA_ROPE // 2
    inv_freq = ROPE_THETA ** (-jnp.arange(half_r, dtype=F32) / half_r)
    ang = positions.reshape(n, 1).astype(F32) * inv_freq
    zeros = jnp.zeros((n, LANES - MLA_ROPE), F32)
    cos_t = jnp.concatenate([jnp.cos(ang), jnp.cos(ang), zeros], axis=1)
    sin_t = jnp.concatenate([-jnp.sin(ang), jnp.sin(ang), zeros], axis=1)

    wuq = l1_mla_w_uq.reshape(q_lora, MLA_HEADS, MLA_QK)
    wuq_p = jnp.concatenate(
        [wuq[..., :MLA_NOPE],
         jnp.pad(wuq[..., MLA_NOPE:], ((0, 0), (0, 0), (0, LANES - MLA_ROPE))),
         jnp.pad(_swap_halves(wuq[..., MLA_NOPE:]), ((0, 0), (0, 0), (0, LANES - MLA_ROPE)))],
        axis=-1).reshape(q_lora, MLA_HEADS * 3 * LANES).astype(BF16)
    q_c, k_c, v_c = _mla_prep(z1, cos_t, sin_t, l1_mla_cq_norm.reshape(1, -1), l1_mla_ckv_norm.reshape(1, -1),
                              wuq_p, l1_mla_w_ukv.astype(BF16), _rope_gains(l1_mla_q_norm),
                              _rope_gains(l1_mla_k_norm), col0=3 * MOBA_HEADS)
    out_c = _mla_attention(q_c, k_c, v_c, batch, seq)

    nb = seq // MOBA_BLOCK
    qn, kn, vb, kmean = _moba_norm(z1, l1_moba_q_norm, l1_moba_k_norm,
                                   col_q=0, col_k=MOBA_HEADS, col_v=2 * MOBA_HEADS, nb=nb)
    out_d = _moba_attention(qn, kn, vb, kmean.reshape(batch, nb, dw), batch, nb)

    x2d = _mixer_out_and_peer_ffn(x2d, out_c, out_d, l1_w_out, l1_norm_ffn, l1_peer_w_query, l1_peer_sub_keys,
                                  l1_peer_expert_down, l1_peer_expert_up)
    return x2d.reshape(batch, seq, d)
```
